```python
import jax, jax.numpy as jnp
from jax import lax
import numpy as np

D_MODEL = 1024
BATCH = 8
SEQ = 4096
DEPTH = 2

CHUNK = 64
N_MIXERS = 2
N_RWKV = (DEPTH + N_MIXERS - 1) // N_MIXERS
N_GLA = DEPTH // N_MIXERS
D_FF = 4 * D_MODEL
NORM_EPS = 1e-5

RWKV_HEAD = 64
RWKV_HEADS = D_MODEL // RWKV_HEAD
N_SHIFT_MIX = 6
DECAY_LORA = max(32, int(round(1.8 * D_MODEL ** 0.5 / 32)) * 32)
A_LORA = max(32, int(round(1.8 * D_MODEL ** 0.5 / 32)) * 32)
GATE_LORA = max(32, int(round(0.6 * D_MODEL ** 0.8 / 32)) * 32)
GN_EPS = 1e-5 * RWKV_HEAD

GLA_HEADS = 4
GLA_QK = D_MODEL // 2
GLA_V = D_MODEL
GLA_DK = GLA_QK // GLA_HEADS
GLA_DV = GLA_V // GLA_HEADS
GLA_GATE_LORA = 16
GLA_GATE_NORMALIZER = 16.0
GLA_IN = 2 * GLA_QK + 2 * GLA_V + GLA_GATE_LORA

kernel_name = "rwkv7_gla_interleaved_hybrid"


def rms_norm(x, g, eps=NORM_EPS):
    xf = x.astype(jnp.float32)
    y = xf * lax.rsqrt(jnp.mean(xf * xf, axis=-1, keepdims=True) + eps)
    return (y * g.astype(jnp.float32)).astype(x.dtype)


def token_shift(x):
    return jnp.pad(x, ((0, 0), (1, 0), (0, 0)))[:, :-1]


def sq_relu_mlp(x, w_up, w_down):
    h = jax.nn.relu(x @ w_up)
    return (h * h) @ w_down


def rwkv7_step(state, inp):
    r_t, w_t, k_t, v_t, a_t, b_t = inp
    sa = jnp.einsum('bhvk,bhk->bhv', state, a_t)
    state = (state * w_t[:, :, None, :]
             + sa[..., None] * b_t[:, :, None, :]
             + v_t[..., None] * k_t[:, :, None, :])
    y = jnp.einsum('bhvk,bhk->bhv', state, r_t)
    return state, y


def rwkv7_time_mix(x, mu, w_rkv, w0, w1, w2, a0, a1, a2, g1, g2, k_k, k_a, r_k, lnx_w, lnx_b, w_o):
    b, s, d = x.shape
    f32 = jnp.float32
    xx = token_shift(x) - x
    xm = x[None] + xx[None] * mu[:, None, None, :]
    rkv = jnp.einsum('nbsd,nde->nbse', xm[:3], w_rkv)
    r, k, v = rkv[0], rkv[1], rkv[2]
    xw, xa, xg = xm[3], xm[4], xm[5]
    w_log = -jax.nn.softplus(-(w0 + jnp.tanh(xw @ w1) @ w2).astype(f32)) - 0.5
    decay = jnp.exp(-jnp.exp(w_log))
    a = jax.nn.sigmoid((a0 + (xa @ a1) @ a2).astype(f32))
    g = jax.nn.sigmoid(xg @ g1) @ g2

    def heads(t):
        return t.reshape(b, s, RWKV_HEADS, RWKV_HEAD)

    kk = heads((k * k_k).astype(f32))
    kk = kk / jnp.maximum(jnp.linalg.norm(kk, axis=-1, keepdims=True), 1e-12)
    k_mod = k.astype(f32) * (1.0 + (a - 1.0) * k_a)
    r_h, k_h, v_h = heads(r.astype(f32)), heads(k_mod), heads(v.astype(f32))
    w_h, a_h = heads(decay), heads(a)

    def tm(t):
        return jnp.moveaxis(t, 1, 0)

    xs = (tm(r_h), tm(w_h), tm(k_h), tm(v_h), tm(-kk), tm(kk * a_h))
    s0 = jnp.zeros((b, RWKV_HEADS, RWKV_HEAD, RWKV_HEAD), f32)
    _, y = lax.scan(rwkv7_step, s0, xs)
    y = jnp.moveaxis(y, 0, 1)
    mean = jnp.mean(y, axis=-1, keepdims=True)
    var = jnp.mean(jnp.square(y - mean), axis=-1, keepdims=True)
    y = ((y - mean) * lax.rsqrt(var + GN_EPS)).reshape(b, s, d) * lnx_w + lnx_b
    bonus = jnp.sum(r_h * k_h * r_k, axis=-1, keepdims=True) * v_h
    y = (y + bonus.reshape(b, s, d)) * g
    return y.astype(x.dtype) @ w_o


def gla_time_mix(x, w_in, w_gk2, b_gk2, gnorm_g, w_o):
    b, s, d = x.shape
    f32 = jnp.float32
    nc = s // CHUNK
    proj = x @ w_in
    q, k, v, og, gk_low = jnp.split(
        proj, [GLA_QK, 2 * GLA_QK, 2 * GLA_QK + GLA_V, 2 * GLA_QK + 2 * GLA_V], axis=-1)
    gk = jax.nn.log_sigmoid((gk_low @ w_gk2 + b_gk2).astype(f32)) / GLA_GATE_NORMALIZER
    q = q * (GLA_DK ** -0.5)

    def chunks(t, dh):
        return t.astype(f32).reshape(b, nc, CHUNK, GLA_HEADS, dh).transpose(1, 0, 3, 2, 4)

    causal = jnp.tril(jnp.ones((CHUNK, CHUNK), dtype=bool))

    def step(state, inp):
        qc, kc, vc, gc = inp
        cum = jnp.cumsum(gc, axis=2)
        inter = jnp.einsum('bhtk,bhkv->bhtv', qc * jnp.exp(cum), state)
        diff = cum[:, :, :, None, :] - cum[:, :, None, :, :]
        decay = jnp.exp(jnp.where(causal[:, :, None], diff, -jnp.inf))
        scores = jnp.einsum('bhtk,bhsk,bhtsk->bhts', qc, kc, decay)
        intra = jnp.einsum('bhts,bhsv->bhtv', scores, vc)
        last = cum[:, :, -1:, :]
        state = (jnp.exp(last[:, :, 0, :])[..., None] * state
                 + jnp.einsum('bhsk,bhsv->bhkv', kc * jnp.exp(last - cum), vc))
        return state, inter + intra

    s0 = jnp.zeros((b, GLA_HEADS, GLA_DK, GLA_DV), f32)
    _, o = lax.scan(step, s0, (chunks(q, GLA_DK), chunks(k, GLA_DK),
                               chunks(v, GLA_DV), chunks(gk, GLA_DK)))
    o = o.transpose(1, 0, 3, 2, 4).reshape(b, s, GLA_HEADS, GLA_DV)
    o = o * lax.rsqrt(jnp.mean(o * o, axis=-1, keepdims=True) + NORM_EPS) * gnorm_g.astype(f32)
    o = o.reshape(b, s, d) * jax.nn.silu(og.astype(f32))
    return o.astype(x.dtype) @ w_o


def setup_inputs(seed: int = 0) -> dict:
    key = jax.random.key(seed)
    ks = jax.random.split(key, 32)
    f32 = jnp.float32

    def nrm(k, shape, scale):
        return jax.random.normal(k, shape, f32) * scale

    D = D_MODEL
    return {
        "x": nrm(ks[0], (BATCH, SEQ, D), 1.0),
        "norm_mix_g": 1.0 + nrm(ks[1], (DEPTH, D), 0.02),
        "norm_ffn_g": 1.0 + nrm(ks[2], (DEPTH, D), 0.02),
        "mlp_up": nrm(ks[3], (DEPTH, D, D_FF), D ** -0.5),
        "mlp_down": nrm(ks[4], (DEPTH, D_FF, D), D_FF ** -0.5),
        "rwkv_mu": jax.random.uniform(ks[5], (N_RWKV, N_SHIFT_MIX, D), f32),
        "rwkv_w_rkv": nrm(ks[6], (N_RWKV, 3, D, D), D ** -0.5),
        "rwkv_w0": nrm(ks[7], (N_RWKV, D), 1.0),
        "rwkv_w1": nrm(ks[8], (N_RWKV, D, DECAY_LORA), D ** -0.5),
        "rwkv_w2": nrm(ks[9], (N_RWKV, DECAY_LORA, D), 0.1 * DECAY_LORA ** -0.5),
        "rwkv_a0": nrm(ks[10], (N_RWKV, D), 0.1),
        "rwkv_a1": nrm(ks[11], (N_RWKV, D, A_LORA), D ** -0.5),
        "rwkv_a2": nrm(ks[12], (N_RWKV, A_LORA, D), 0.1 * A_LORA ** -0.5),
        "rwkv_g1": nrm(ks[13], (N_RWKV, D, GATE_LORA), D ** -0.5),
        "rwkv_g2": nrm(ks[14], (N_RWKV, GATE_LORA, D), GATE_LORA ** -0.5),
        "rwkv_k_k": 0.85 + nrm(ks[15], (N_RWKV, D), 0.02),
        "rwkv_k_a": 1.0 + nrm(ks[16], (N_RWKV, D), 0.02),
        "rwkv_r_k": nrm(ks[17], (N_RWKV, RWKV_HEADS, RWKV_HEAD), 0.1),
        "rwkv_lnx_w": 1.0 + nrm(ks[18], (N_RWKV, D), 0.02),
        "rwkv_lnx_b": nrm(ks[19], (N_RWKV, D), 0.02),
        "rwkv_w_o": nrm(ks[20], (N_RWKV, D, D), D ** -0.5),
        "gla_w_in": nrm(ks[21], (N_GLA, D, GLA_IN), D ** -0.5),
        "gla_w_gk2": nrm(ks[22], (N_GLA, GLA_GATE_LORA, GLA_QK), GLA_GATE_LORA ** -0.5),
        "gla_b_gk2": nrm(ks[23], (N_GLA, GLA_QK), 0.1),
        "gla_gnorm_g": 1.0 + nrm(ks[24], (N_GLA, GLA_DV), 0.02),
        "gla_w_o": nrm(ks[25], (N_GLA, GLA_V, D), GLA_V ** -0.5),
        "final_g": 1.0 + nrm(ks[26], (D,), 0.02),
    }


def reference(x, norm_mix_g, norm_ffn_g, mlp_up, mlp_down,
              rwkv_mu, rwkv_w_rkv, rwkv_w0, rwkv_w1, rwkv_w2, rwkv_a0, rwkv_a1, rwkv_a2,
              rwkv_g1, rwkv_g2, rwkv_k_k, rwkv_k_a, rwkv_r_k, rwkv_lnx_w, rwkv_lnx_b, rwkv_w_o,
              gla_w_in, gla_w_gk2, gla_b_gk2, gla_gnorm_g, gla_w_o, final_g):
    h = x
    for i in range(DEPTH):
        hn = rms_norm(h, norm_mix_g[i])
        j = i // N_MIXERS
        if i % N_MIXERS == 0:
            mix = rwkv7_time_mix(hn, rwkv_mu[j], rwkv_w_rkv[j], rwkv_w0[j], rwkv_w1[j], rwkv_w2[j],
                                 rwkv_a0[j], rwkv_a1[j], rwkv_a2[j], rwkv_g1[j], rwkv_g2[j],
                                 rwkv_k_k[j], rwkv_k_a[j], rwkv_r_k[j], rwkv_lnx_w[j],
                                 rwkv_lnx_b[j], rwkv_w_o[j])
        else:
            mix = gla_time_mix(hn, gla_w_in[j], gla_w_gk2[j], gla_b_gk2[j], gla_gnorm_g[j], gla_w_o[j])
        h = h + mix
        h = h + sq_relu_mlp(rms_norm(h, norm_ffn_g[i]), mlp_up[i], mlp_down[i])
    return rms_norm(h, final_g)
```

```python
import functools
import math

import jax
import jax.numpy as jnp
from jax import lax
from jax.experimental import pallas as pl
from jax.experimental.pallas import tpu as pltpu

F32 = jnp.float32
BF16 = jnp.bfloat16

CHUNK = 64
RWKV_HEAD = 64
GLA_HEADS = 4
GLA_GATE_NORMALIZER = 16.0
GLA_SUB = 16
NORM_EPS = 1e-5
GN_EPS = 1e-5 * RWKV_HEAD
V7X_VMEM_LIMIT_BYTES = 56 * 1024 * 1024


def _dot(a, b):
    return jnp.dot(a.astype(BF16), b.astype(BF16), preferred_element_type=F32)


def _dot_nt(a, b):
    return lax.dot_general(a.astype(BF16), b.astype(BF16), (((1,), (1,)), ((), ())),
                           preferred_element_type=F32)


def _dot_tn(a, b):
    return lax.dot_general(a.astype(BF16), b.astype(BF16), (((0,), (0,)), ((), ())),
                           preferred_element_type=F32)


def _dot_split(a, b):
    hi = a.astype(BF16)
    lo = (a - hi.astype(F32)).astype(BF16)
    bb = b.astype(BF16)
    return (jnp.dot(hi, bb, preferred_element_type=F32)
            + jnp.dot(lo, bb, preferred_element_type=F32))


def _rms(x, g):
    return x * lax.rsqrt(jnp.mean(x * x, axis=-1, keepdims=True) + NORM_EPS) * g


def _cumsum_rows(x, tri):
    hi = x.astype(BF16)
    lo = (x - hi.astype(F32)).astype(BF16)
    return (jnp.dot(tri, hi, preferred_element_type=F32)
            + jnp.dot(tri, lo, preferred_element_type=F32))


def _const_spec(shape):
    n = len(shape)
    return pl.BlockSpec(shape, lambda *_: (0,) * n)


def _params(sem):
    return pltpu.CompilerParams(dimension_semantics=sem,
                                vmem_limit_bytes=V7X_VMEM_LIMIT_BYTES)


def _rwkv_pre_kernel(x_ref, xh_ref, g_ref, mu_ref, wrkv_ref, w0_ref, w1_ref, w2_ref,
                     a0_ref, a1_ref, a2_ref, g1_ref, g2_ref, kk_ref, ka_ref, e_ref, et_ref,
                     r_out, lw_out, k_out, v_out, a_out, b_out, gate_out, *, tiles_per_seq):
    i = pl.program_id(0)
    g = g_ref[...]
    x = x_ref[...]
    tm = x.shape[0]
    hn = _rms(x, g)
    hp = _rms(xh_ref[...], g)[7:8, :]
    hp = jnp.where(i % tiles_per_seq == 0, 0.0, hp)
    row = lax.broadcasted_iota(jnp.int32, (tm, 1), 0)
    sh = jnp.where(row == 0, hp, pltpu.roll(hn, 1, axis=0))
    xx = sh - hn
    mu = mu_ref[...]

    def mix(n):
        return (hn + xx * mu[n:n + 1, :]).astype(BF16)

    r = jnp.dot(mix(0), wrkv_ref[0], preferred_element_type=F32)
    k = jnp.dot(mix(1), wrkv_ref[1], preferred_element_type=F32)
    v = jnp.dot(mix(2), wrkv_ref[2], preferred_element_type=F32)
    zw = w0_ref[...] + _dot(jnp.tanh(_dot(mix(3), w1_ref[...])), w2_ref[...])
    lw = -math.exp(-0.5) * jax.nn.sigmoid(zw)
    asig = jax.nn.sigmoid(a0_ref[...] + _dot(_dot(mix(4), a1_ref[...]), a2_ref[...]))
    gate = _dot(jax.nn.sigmoid(_dot(mix(5), g1_ref[...])), g2_ref[...])

    kk = k * kk_ref[...]
    ss = _dot(kk * kk, e_ref[...])
    inv = 1.0 / jnp.maximum(jnp.sqrt(ss), 1e-12)
    kkn = kk * _dot_split(inv, et_ref[...])
    kmod = k * (1.0 + (asig - 1.0) * ka_ref[...])

    r_out[...] = r.astype(r_out.dtype)
    lw_out[...] = lw
    k_out[...] = kmod.astype(k_out.dtype)
    v_out[...] = v.astype(v_out.dtype)
    a_out[...] = (-kkn).astype(a_out.dtype)
    b_out[...] = (kkn * asig).astype(b_out.dtype)
    gate_out[...] = gate.astype(gate_out.dtype)


def _rwkv_pre(x2, seq, g, mu, wrkv, w0, w1, w2, a0, a1, a2, g1, g2, k_k, k_a, e, et, tm=256):
    m, d = x2.shape
    tiles_per_seq = seq // tm
    row_spec = pl.BlockSpec((tm, d), lambda i: (i, 0))
    halo_spec = pl.BlockSpec((8, d), lambda i: (jnp.maximum(i * (tm // 8) - 1, 0), 0))
    consts = (g, mu, wrkv, w0, w1, w2, a0, a1, a2, g1, g2, k_k, k_a, e, et)
    out_bf = jax.ShapeDtypeStruct((m, d), BF16)
    out_f32 = jax.ShapeDtypeStruct((m, d), F32)
    return pl.pallas_call(
        functools.partial(_rwkv_pre_kernel, tiles_per_seq=tiles_per_seq),
        grid=(m // tm,),
        in_specs=[row_spec, halo_spec] + [_const_spec(c.shape) for c in consts],
        out_specs=[row_spec] * 7,
        out_shape=[out_bf, out_f32, out_bf, out_bf, out_bf, out_bf, out_bf],
        compiler_params=_params(("arbitrary",)),
        name="rwkv_pre",
    )(x2, x2, *consts)


def _rwkv_scan_kernel(r_ref, lw_ref, k_ref, v_ref, a_ref, b_ref, y_ref, h_ref):
    c = pl.program_id(1)

    @pl.when(c == 0)
    def _():
        h_ref[...] = jnp.zeros_like(h_ref)

    L = lw_ref.shape[0]
    n = RWKV_HEAD
    heads = lw_ref.shape[1] // n
    row = lax.broadcasted_iota(jnp.int32, (L, L), 0)
    col = lax.broadcasted_iota(jnp.int32, (L, L), 1)
    strict = row > col
    incl = row >= col
    eye = row == col
    tri = jnp.where(incl, 1.0, 0.0).astype(BF16)

    lw = lw_ref[...]
    cl = _cumsum_rows(lw, tri)
    cl_last = cl[L - 1:L, :]
    p = jnp.exp(cl)
    ip = jnp.exp(-cl)
    pprev = jnp.exp(cl - lw)
    pl_rel = jnp.exp(cl_last - cl)
    p_last = jnp.exp(cl_last)
    r = r_ref[...].astype(F32)
    k = k_ref[...].astype(F32)
    a = a_ref[...].astype(F32)
    b = b_ref[...].astype(F32)
    v_all = v_ref[...]
    rt_all = r * p
    at_all = a * pprev
    bt_all = b * ip
    kt_all = k * ip
    bh_all = b * pl_rel
    kh_all = k * pl_rel

    ys = []
    for h in range(heads):
        sl = slice(h * n, (h + 1) * n)
        rt, at, bt, kt = rt_all[:, sl], at_all[:, sl], bt_all[:, sl], kt_all[:, sl]
        bh, kh, v = bh_all[:, sl], kh_all[:, sl], v_all[:, sl]
        aab = jnp.where(strict, _dot_nt(at, bt), 0.0)
        aak = jnp.where(strict, _dot_nt(at, kt), 0.0)
        mrb = jnp.where(incl, _dot_nt(rt, bt), 0.0)
        mrk = jnp.where(incl, _dot_nt(rt, kt), 0.0)
        z = jnp.concatenate([at, _dot(aak, v)], axis=1)
        pw = aab
        steps = int(math.log2(L))
        for it in range(steps):
            if it < steps - 1:
                w = _dot(pw, jnp.concatenate([z, pw], axis=1))
                z = z + w[:, :2 * n]
                pw = w[:, 2 * n:]
            else:
                z = z + _dot(pw, z)
        ah, u0 = z[:, :n], z[:, n:]
        rh = rt + _dot(mrb, ah)
        y0 = _dot(mrb, u0) + _dot(mrk, v)
        gm = _dot_tn(bh, ah) + jnp.where(eye, p_last[:, sl], 0.0)
        cm = _dot_tn(bh, u0) + _dot_tn(kh, v)
        hs = h_ref[h]
        ys.append(_dot(rh, hs) + y0)
        h_ref[h] = _dot(gm, hs) + cm
    y_ref[...] = jnp.concatenate(ys, axis=1)


def _rwkv_scan(r, lw, k, v, a, b, batch, seq):
    m, d = lw.shape
    nc = seq // CHUNK
    spec = pl.BlockSpec((CHUNK, d), lambda bi, ci: (bi * nc + ci, 0))
    heads = d // RWKV_HEAD
    return pl.pallas_call(
        _rwkv_scan_kernel,
        grid=(batch, nc),
        in_specs=[spec] * 6,
        out_specs=spec,
        out_shape=jax.ShapeDtypeStruct((m, d), F32),
        scratch_shapes=[pltpu.VMEM((heads, RWKV_HEAD, RWKV_HEAD), F32)],
        compiler_params=_params(("arbitrary", "arbitrary")),
        name="rwkv_scan",
    )(r, lw, k, v, a, b)


def _rwkv_post_kernel(y_ref, r_ref, k_ref, v_ref, gate_ref, x_ref, lnw_ref, lnb_ref, rk_ref,
                      wo_ref, e_ref, et_ref, h_out):
    e = e_ref[...]
    et = et_ref[...]
    inv_n = 1.0 / RWKV_HEAD
    y = y_ref[...]
    mean = _dot_split(_dot(y, e) * inv_n, et)
    yc = y - mean
    var = _dot(yc * yc, e) * inv_n
    yn = yc * _dot_split(lax.rsqrt(var + GN_EPS), et) * lnw_ref[...] + lnb_ref[...]
    r = r_ref[...].astype(F32)
    k = k_ref[...].astype(F32)
    bonus = _dot_split(_dot(r * k * rk_ref[...], e), et) * v_ref[...].astype(F32)
    out = (yn + bonus) * gate_ref[...].astype(F32)
    h_out[...] = x_ref[...] + _dot(out, wo_ref[...])


def _rwkv_post(y, r, k, v, gate, x2, lnw, lnb, rk, wo, e, et, tm=512):
    m, d = y.shape
    row_spec = pl.BlockSpec((tm, d), lambda i: (i, 0))
    consts = (lnw, lnb, rk, wo, e, et)
    return pl.pallas_call(
        _rwkv_post_kernel,
        grid=(m // tm,),
        in_specs=[row_spec] * 6 + [_const_spec(c.shape) for c in consts],
        out_specs=row_spec,
        out_shape=jax.ShapeDtypeStruct((m, d), F32),
        compiler_params=_params(("arbitrary",)),
        name="rwkv_post",
    )(y, r, k, v, gate, x2, *consts)


def _mlp(h, g, wup_ref, wdn_ref):
    d = h.shape[1]
    xn = _rms(h, g).astype(BF16)
    acc = h
    for j in range(wup_ref.shape[1] // d):
        up = jnp.dot(xn, wup_ref[:, j * d:(j + 1) * d], preferred_element_type=F32)
        up = jnp.maximum(up, 0.0)
        acc = acc + jnp.dot((up * up).astype(BF16), wdn_ref[j * d:(j + 1) * d, :],
                            preferred_element_type=F32)
    return acc


def _mlp_gla_in_kernel(h_ref, gf_ref, wup_ref, wdn_ref, gm_ref, wq_ref, wk_ref, wv_ref, wg_ref,
                       wl_ref, wgk2_ref, bgk2_ref, h_out, q_out, k_out, v_out, og_out, gk_out):
    h2 = _mlp(h_ref[...], gf_ref[...], wup_ref, wdn_ref)
    h_out[...] = h2
    hn = _rms(h2, gm_ref[...]).astype(BF16)
    q_out[...] = jnp.dot(hn, wq_ref[...], preferred_element_type=F32).astype(q_out.dtype)
    k_out[...] = jnp.dot(hn, wk_ref[...], preferred_element_type=F32).astype(k_out.dtype)
    v_out[...] = jnp.dot(hn, wv_ref[...], preferred_element_type=F32).astype(v_out.dtype)
    og_out[...] = jnp.dot(hn, wg_ref[...], preferred_element_type=F32).astype(og_out.dtype)
    low = jnp.dot(hn, wl_ref[...], preferred_element_type=F32)
    z = _dot_split(low, wgk2_ref[...]) + bgk2_ref[...]
    gk_out[...] = jax.nn.log_sigmoid(z) * (1.0 / GLA_GATE_NORMALIZER)


def _mlp_gla_in(h1, gf, wup, wdn, gm, wq, wk, wv, wg, wl, wgk2, bgk2, tm=512):
    m, d = h1.shape
    dqk = wq.shape[1]
    row = lambda w: pl.BlockSpec((tm, w), lambda i: (i, 0))
    consts = (gf, wup, wdn, gm, wq, wk, wv, wg, wl, wgk2, bgk2)
    return pl.pallas_call(
        _mlp_gla_in_kernel,
        grid=(m // tm,),
        in_specs=[row(d)] + [_const_spec(c.shape) for c in consts],
        out_specs=[row(d), row(dqk), row(dqk), row(d), row(d), row(dqk)],
        out_shape=[jax.ShapeDtypeStruct((m, d), F32),
                   jax.ShapeDtypeStruct((m, dqk), BF16),
                   jax.ShapeDtypeStruct((m, dqk), BF16),
                   jax.ShapeDtypeStruct((m, d), BF16),
                   jax.ShapeDtypeStruct((m, d), BF16),
                   jax.ShapeDtypeStruct((m, dqk), F32)],
        compiler_params=_params(("arbitrary",)),
        name="mlp_gla_in",
    )(h1, *consts)


def _gla_scan_kernel(q_ref, k_ref, v_ref, g_ref, gn_ref, o_ref, s_ref):
    c = pl.program_id(1)

    @pl.when(c == 0)
    def _():
        s_ref[...] = jnp.zeros_like(s_ref)

    L = g_ref.shape[0]
    dk = g_ref.shape[1] // GLA_HEADS
    dv = v_ref.shape[1] // GLA_HEADS
    half, quarter = L // 2, L // 4
    row = lax.broadcasted_iota(jnp.int32, (L, L), 0)
    col = lax.broadcasted_iota(jnp.int32, (L, L), 1)
    tri = jnp.where(row >= col, 1.0, 0.0).astype(BF16)
    rblk, cblk = row // GLA_SUB, col // GLA_SUB
    diag_off = jnp.where(rblk == cblk, row - col, -1)
    mask_a = (row // half - col // half) == 1
    mask_b = jnp.where(rblk % 2 == 1, rblk - cblk, 0) == 1
    rowv = lax.broadcasted_iota(jnp.int32, (L, 1), 0)

    cum = _cumsum_rows(g_ref[...], tri)
    last = cum[L - 1:L, :]
    ref_a = cum[half - 1:half, :]
    ref_b = jnp.where(rowv < half, cum[quarter - 1:quarter, :], cum[half + quarter - 1:half + quarter, :])
    q_all = q_ref[...].astype(F32) * (dk ** -0.5)
    k_all = k_ref[...].astype(F32)
    gn = gn_ref[...]

    for h in range(GLA_HEADS):
        sk = slice(h * dk, (h + 1) * dk)
        sv = slice(h * dv, (h + 1) * dv)
        q, k, g = q_all[:, sk], k_all[:, sk], cum[:, sk]
        v = v_ref[:, sv]
        st = s_ref[h]
        inter = _dot_nt(q * jnp.exp(g), st)
        k_dec = k * jnp.exp(last[:, sk] - g)
        s_ref[h] = st * jnp.exp(last[:, sk]) + _dot_tn(v, k_dec)

        ra, rb = ref_a[:, sk], ref_b[:, sk]
        s_a = _dot_nt(q * jnp.exp(jnp.minimum(g - ra, 0.0)), k * jnp.exp(jnp.minimum(ra - g, 0.0)))
        s_b = _dot_nt(q * jnp.exp(jnp.minimum(g - rb, 0.0)), k * jnp.exp(jnp.minimum(rb - g, 0.0)))
        s_d = jnp.zeros((L, L), F32)
        for dlt in range(GLA_SUB):
            if dlt == 0:
                kr, gr = k, g
            else:
                kr, gr = pltpu.roll(k, dlt, axis=0), pltpu.roll(g, dlt, axis=0)
            e = jnp.exp(jnp.minimum(g - gr, 0.0))
            dsum = jnp.sum(q * kr * e, axis=-1, keepdims=True)
            s_d = jnp.where(diag_off == dlt, dsum, s_d)
        scores = jnp.where(mask_a, s_a, jnp.where(mask_b, s_b, s_d))
        o = inter + _dot(scores, v)
        o = o * lax.rsqrt(jnp.mean(o * o, axis=-1, keepdims=True) + NORM_EPS) * gn
        o_ref[:, sv] = o.astype(o_ref.dtype)


def _gla_scan(q, k, v, gk, gn, batch, seq):
    m, dqk = q.shape
    d = v.shape[1]
    nc = seq // CHUNK
    idx = lambda bi, ci: (bi * nc + ci, 0)
    return pl.pallas_call(
        _gla_scan_kernel,
        grid=(batch, nc),
        in_specs=[pl.BlockSpec((CHUNK, dqk), idx), pl.BlockSpec((CHUNK, dqk), idx),
                  pl.BlockSpec((CHUNK, d), idx), pl.BlockSpec((CHUNK, dqk), idx),
                  _const_spec(gn.shape)],
        out_specs=pl.BlockSpec((CHUNK, d), idx),
        out_shape=jax.ShapeDtypeStruct((m, d), BF16),
        scratch_shapes=[pltpu.VMEM((GLA_HEADS, d // GLA_HEADS, dqk // GLA_HEADS), F32)],
        compiler_params=_params(("arbitrary", "arbitrary")),
        name="gla_scan",
    )(q, k, v, gk, gn)


def _gla_post_kernel(o_ref, og_ref, h_ref, wo_ref, gf_ref, wup_ref, wdn_ref, fg_ref, out_ref):
    og = og_ref[...].astype(F32)
    gated = o_ref[...].astype(F32) * (og * jax.nn.sigmoid(og))
    h3 = h_ref[...] + _dot(gated, wo_ref[...])
    h4 = _mlp(h3, gf_ref[...], wup_ref, wdn_ref)
    out_ref[...] = _rms(h4, fg_ref[...])


def _gla_post(o, og, h2, wo, gf, wup, wdn, fg, tm=512):
    m, d = h2.shape
    row_spec = pl.BlockSpec((tm, d), lambda i: (i, 0))
    consts = (wo, gf, wup, wdn, fg)
    return pl.pallas_call(
        _gla_post_kernel,
        grid=(m // tm,),
        in_specs=[row_spec] * 3 + [_const_spec(c.shape) for c in consts],
        out_specs=row_spec,
        out_shape=jax.ShapeDtypeStruct((m, d), F32),
        compiler_params=_params(("arbitrary",)),
        name="gla_post",
    )(o, og, h2, *consts)


def kernel(x, norm_mix_g, norm_ffn_g, mlp_up, mlp_down, rwkv_mu, rwkv_w_rkv, rwkv_w0, rwkv_w1, rwkv_w2, rwkv_a0, rwkv_a1, rwkv_a2, rwkv_g1, rwkv_g2, rwkv_k_k, rwkv_k_a, rwkv_r_k, rwkv_lnx_w, rwkv_lnx_b, rwkv_w_o, gla_w_in, gla_w_gk2, gla_b_gk2, gla_gnorm_g, gla_w_o, final_g):
    batch, seq, d = x.shape
    assert seq % CHUNK == 0 and d % RWKV_HEAD == 0
    m = batch * seq
    heads = d // RWKV_HEAD
    x2 = x.reshape(m, d)
    row = lambda t: t.reshape(1, -1).astype(F32)
    bf = lambda t: t.astype(BF16)
    head_of = jnp.arange(d, dtype=jnp.int32) // RWKV_HEAD
    e = (head_of[:, None] == jnp.arange(heads, dtype=jnp.int32)[None, :]).astype(BF16)
    et = e.T

    r, lw, kmod, v, a, b, gate = _rwkv_pre(
        x2, seq, row(norm_mix_g[0]), rwkv_mu[0], bf(rwkv_w_rkv[0]), row(rwkv_w0[0]),
        bf(rwkv_w1[0]), bf(rwkv_w2[0]), row(rwkv_a0[0]), bf(rwkv_a1[0]), bf(rwkv_a2[0]),
        bf(rwkv_g1[0]), bf(rwkv_g2[0]), row(rwkv_k_k[0]), row(rwkv_k_a[0]), e, et)
    y = _rwkv_scan(r, lw, kmod, v, a, b, batch, seq)
    h1 = _rwkv_post(y, r, kmod, v, gate, x2, row(rwkv_lnx_w[0]), row(rwkv_lnx_b[0]),
                    row(rwkv_r_k[0]), bf(rwkv_w_o[0]), e, et)

    w_in = gla_w_in[0]
    dqk = gla_w_gk2.shape[2]
    o_q, o_k, o_v, o_g = 0, dqk, 2 * dqk, 2 * dqk + d
    h2, q, k, vv, og, gk = _mlp_gla_in(
        h1, row(norm_ffn_g[0]), bf(mlp_up[0]), bf(mlp_down[0]), row(norm_mix_g[1]),
        bf(w_in[:, o_q:o_k]), bf(w_in[:, o_k:o_v]), bf(w_in[:, o_v:o_g]),
        bf(w_in[:, o_g:o_g + d]), bf(w_in[:, o_g + d:]), gla_w_gk2[0].astype(F32),
        row(gla_b_gk2[0]))
    o = _gla_scan(q, k, vv, gk, row(gla_gnorm_g[0]), batch, seq)
    out = _gla_post(o, og, h2, bf(gla_w_o[0]), row(norm_ffn_g[1]), bf(mlp_up[1]),
                    bf(mlp_down[1]), row(final_g))
    return out.reshape(batch, seq, d)
```

```python
import functools
import math

import jax
import jax.numpy as jnp
from jax import lax
from jax.experimental import pallas as pl
from jax.experimental.pallas import tpu as pltpu

F32 = jnp.float32
BF16 = jnp.bfloat16

CHUNK = 64
RWKV_HEAD = 64
GLA_HEADS = 4
GLA_GATE_NORMALIZER = 16.0
GLA_SUB = 16
RWKV_HEAD_GROUP = 16
NORM_EPS = 1e-5
GN_EPS = 1e-5 * RWKV_HEAD
V7X_VMEM_LIMIT_BYTES = 56 * 1024 * 1024


def _dot(a, b):
    return jnp.dot(a.astype(BF16), b.astype(BF16), preferred_element_type=F32)


def _dot_nt(a, b):
    return lax.dot_general(a.astype(BF16), b.astype(BF16), (((1,), (1,)), ((), ())),
                           preferred_element_type=F32)


def _dot_tn(a, b):
    return lax.dot_general(a.astype(BF16), b.astype(BF16), (((0,), (0,)), ((), ())),
                           preferred_element_type=F32)


def _dot_split(a, b):
    hi = a.astype(BF16)
    lo = (a - hi.astype(F32)).astype(BF16)
    bb = b.astype(BF16)
    return (jnp.dot(hi, bb, preferred_element_type=F32)
            + jnp.dot(lo, bb, preferred_element_type=F32))


def _rms(x, g):
    return x * lax.rsqrt(jnp.mean(x * x, axis=-1, keepdims=True) + NORM_EPS) * g


def _cumsum_rows(x, tri):
    hi = x.astype(BF16)
    lo = (x - hi.astype(F32)).astype(BF16)
    return (jnp.dot(tri, hi, preferred_element_type=F32)
            + jnp.dot(tri, lo, preferred_element_type=F32))


def _const_spec(shape):
    n = len(shape)
    return pl.BlockSpec(shape, lambda *_: (0,) * n)


def _params(sem):
    return pltpu.CompilerParams(dimension_semantics=sem,
                                vmem_limit_bytes=V7X_VMEM_LIMIT_BYTES)


def _rwkv_pre_kernel(x_ref, xh_ref, g_ref, mu_ref, wrkv_ref, w0_ref, w1_ref, w2_ref,
                     a0_ref, a1_ref, a2_ref, g1_ref, g2_ref, kk_ref, ka_ref, e_ref, et_ref,
                     r_out, lw_out, k_out, v_out, a_out, b_out, gate_out, *, tiles_per_seq):
    i = pl.program_id(0)
    g = g_ref[...]
    x = x_ref[...]
    tm = x.shape[0]
    hn = _rms(x, g)
    hp = _rms(xh_ref[...], g)[7:8, :]
    hp = jnp.where(i % tiles_per_seq == 0, 0.0, hp)
    row = lax.broadcasted_iota(jnp.int32, (tm, 1), 0)
    sh = jnp.where(row == 0, hp, pltpu.roll(hn, 1, axis=0))
    xx = sh - hn
    mu = mu_ref[...]

    def mix(n):
        return (hn + xx * mu[n:n + 1, :]).astype(BF16)

    r = jnp.dot(mix(0), wrkv_ref[0], preferred_element_type=F32)
    k = jnp.dot(mix(1), wrkv_ref[1], preferred_element_type=F32)
    v = jnp.dot(mix(2), wrkv_ref[2], preferred_element_type=F32)
    zw = w0_ref[...] + _dot(jnp.tanh(_dot(mix(3), w1_ref[...])), w2_ref[...])
    lw = -math.exp(-0.5) * jax.nn.sigmoid(zw)
    asig = jax.nn.sigmoid(a0_ref[...] + _dot(_dot(mix(4), a1_ref[...]), a2_ref[...]))
    gate = _dot(jax.nn.sigmoid(_dot(mix(5), g1_ref[...])), g2_ref[...])

    kk = k * kk_ref[...]
    ss = _dot(kk * kk, e_ref[...])
    inv = 1.0 / jnp.maximum(jnp.sqrt(ss), 1e-12)
    kkn = kk * _dot_split(inv, et_ref[...])
    kmod = k * (1.0 + (asig - 1.0) * ka_ref[...])

    r_out[...] = r.astype(r_out.dtype)
    lw_out[...] = lw
    k_out[...] = kmod.astype(k_out.dtype)
    v_out[...] = v.astype(v_out.dtype)
    a_out[...] = (-kkn).astype(a_out.dtype)
    b_out[...] = (kkn * asig).astype(b_out.dtype)
    gate_out[...] = gate.astype(gate_out.dtype)


def _rwkv_pre(x2, seq, g, mu, wrkv, w0, w1, w2, a0, a1, a2, g1, g2, k_k, k_a, e, et, tm=256):
    m, d = x2.shape
    tiles_per_seq = seq // tm
    row_spec = pl.BlockSpec((tm, d), lambda i: (i, 0))
    halo_spec = pl.BlockSpec((8, d), lambda i: (jnp.maximum(i * (tm // 8) - 1, 0), 0))
    consts = (g, mu, wrkv, w0, w1, w2, a0, a1, a2, g1, g2, k_k, k_a, e, et)
    out_bf = jax.ShapeDtypeStruct((m, d), BF16)
    out_f32 = jax.ShapeDtypeStruct((m, d), F32)
    return pl.pallas_call(
        functools.partial(_rwkv_pre_kernel, tiles_per_seq=tiles_per_seq),
        grid=(m // tm,),
        in_specs=[row_spec, halo_spec] + [_const_spec(c.shape) for c in consts],
        out_specs=[row_spec] * 7,
        out_shape=[out_bf, out_f32, out_bf, out_bf, out_bf, out_bf, out_bf],
        compiler_params=_params(("arbitrary",)),
        name="rwkv_pre",
    )(x2, x2, *consts)


def _rwkv_scan_kernel(r_ref, lw_ref, k_ref, v_ref, a_ref, b_ref, y_ref, h_ref):
    c = pl.program_id(1)

    @pl.when(c == 0)
    def _():
        h_ref[...] = jnp.zeros_like(h_ref)

    L = lw_ref.shape[0]
    n = RWKV_HEAD
    heads = lw_ref.shape[1] // n
    row = lax.broadcasted_iota(jnp.int32, (L, L), 0)
    col = lax.broadcasted_iota(jnp.int32, (L, L), 1)
    strict = row > col
    incl = row >= col
    eye = row == col
    tri = jnp.where(incl, 1.0, 0.0).astype(BF16)

    lw = lw_ref[...]
    cl = _cumsum_rows(lw, tri)
    cl_last = cl[L - 1:L, :]
    p = jnp.exp(cl)
    ip = jnp.exp(-cl)
    pprev = jnp.exp(cl - lw)
    pl_rel = jnp.exp(cl_last - cl)
    p_last = jnp.exp(cl_last)
    r = r_ref[...].astype(F32)
    k = k_ref[...].astype(F32)
    a = a_ref[...].astype(F32)
    b = b_ref[...].astype(F32)
    v_all = v_ref[...]
    rt_all = r * p
    at_all = a * pprev
    bt_all = b * ip
    kt_all = k * ip
    bh_all = b * pl_rel
    kh_all = k * pl_rel

    row2 = lax.broadcasted_iota(jnp.int32, (2 * L, L), 0)
    col2 = lax.broadcasted_iota(jnp.int32, (2 * L, L), 1)
    mask2 = jnp.where(row2 < L, row2, row2 - L + 1) > col2
    steps = int(math.log2(L))

    ys = [None] * heads
    for g0 in range(0, heads, RWKV_HEAD_GROUP):
        hs_ids = list(range(g0, min(g0 + RWKV_HEAD_GROUP, heads)))
        sls = {h: slice(h * n, (h + 1) * n) for h in hs_ids}
        x1 = {h: jnp.concatenate([at_all[:, sls[h]], rt_all[:, sls[h]]], axis=0).astype(BF16)
              for h in hs_ids}
        sb = {h: jnp.where(mask2, _dot_nt(x1[h], bt_all[:, sls[h]]), 0.0) for h in hs_ids}
        sk = {h: jnp.where(mask2, _dot_nt(x1[h], kt_all[:, sls[h]]), 0.0) for h in hs_ids}
        r2 = {h: _dot(jnp.concatenate([sk[h], kh_all[:, sls[h]].T], axis=0), v_all[:, sls[h]])
              for h in hs_ids}
        z = {h: jnp.concatenate([at_all[:, sls[h]], r2[h][:L]], axis=1) for h in hs_ids}
        pw = {h: sb[h][:L] for h in hs_ids}
        for it in range(steps):
            if it < steps - 1:
                w = {h: _dot(pw[h], jnp.concatenate([z[h], pw[h]], axis=1)) for h in hs_ids}
                z = {h: z[h] + w[h][:, :2 * n] for h in hs_ids}
                pw = {h: w[h][:, 2 * n:] for h in hs_ids}
            else:
                z = {h: z[h] + _dot(pw[h], z[h]) for h in hs_ids}
        r9 = {h: _dot(jnp.concatenate([sb[h][L:], bh_all[:, sls[h]].T], axis=0), z[h]) for h in hs_ids}
        lhs10 = {}
        for h in hs_ids:
            rh = rt_all[:, sls[h]] + r9[h][:L, :n]
            gm = r9[h][L:, :n] + jnp.where(eye, p_last[:, sls[h]], 0.0)
            lhs10[h] = jnp.concatenate([rh, gm], axis=0)
        r10 = {h: _dot(lhs10[h], h_ref[h]) for h in hs_ids}
        for h in hs_ids:
            ys[h] = r10[h][:L] + r9[h][:L, n:] + r2[h][L:2 * L]
            h_ref[h] = r10[h][L:] + r9[h][L:, n:] + r2[h][2 * L:]
    y_ref[...] = jnp.concatenate(ys, axis=1)


def _rwkv_scan(r, lw, k, v, a, b, batch, seq):
    m, d = lw.shape
    nc = seq // CHUNK
    spec = pl.BlockSpec((CHUNK, d), lambda bi, ci: (bi * nc + ci, 0))
    heads = d // RWKV_HEAD
    return pl.pallas_call(
        _rwkv_scan_kernel,
        grid=(batch, nc),
        in_specs=[spec] * 6,
        out_specs=spec,
        out_shape=jax.ShapeDtypeStruct((m, d), F32),
        scratch_shapes=[pltpu.VMEM((heads, RWKV_HEAD, RWKV_HEAD), F32)],
        compiler_params=_params(("arbitrary", "arbitrary")),
        name="rwkv_scan",
    )(r, lw, k, v, a, b)


def _rwkv_post_kernel(y_ref, r_ref, k_ref, v_ref, gate_ref, x_ref, lnw_ref, lnb_ref, rk_ref,
                      wo_ref, e_ref, et_ref, h_out):
    e = e_ref[...]
    et = et_ref[...]
    inv_n = 1.0 / RWKV_HEAD
    y = y_ref[...]
    mean = _dot_split(_dot(y, e) * inv_n, et)
    yc = y - mean
    var = _dot(yc * yc, e) * inv_n
    yn = yc * _dot_split(lax.rsqrt(var + GN_EPS), et) * lnw_ref[...] + lnb_ref[...]
    r = r_ref[...].astype(F32)
    k = k_ref[...].astype(F32)
    bonus = _dot_split(_dot(r * k * rk_ref[...], e), et) * v_ref[...].astype(F32)
    out = (yn + bonus) * gate_ref[...].astype(F32)
    h_out[...] = x_ref[...] + _dot(out, wo_ref[...])


def _rwkv_post(y, r, k, v, gate, x2, lnw, lnb, rk, wo, e, et, tm=512):
    m, d = y.shape
    row_spec = pl.BlockSpec((tm, d), lambda i: (i, 0))
    consts = (lnw, lnb, rk, wo, e, et)
    return pl.pallas_call(
        _rwkv_post_kernel,
        grid=(m // tm,),
        in_specs=[row_spec] * 6 + [_const_spec(c.shape) for c in consts],
        out_specs=row_spec,
        out_shape=jax.ShapeDtypeStruct((m, d), F32),
        compiler_params=_params(("arbitrary",)),
        name="rwkv_post",
    )(y, r, k, v, gate, x2, *consts)


def _mlp(h, g, wup_ref, wdn_ref):
    d = h.shape[1]
    xn = _rms(h, g).astype(BF16)
    acc = h
    for j in range(wup_ref.shape[1] // d):
        up = jnp.dot(xn, wup_ref[:, j * d:(j + 1) * d], preferred_element_type=F32)
        up = jnp.maximum(up, 0.0)
        acc = acc + jnp.dot((up * up).astype(BF16), wdn_ref[j * d:(j + 1) * d, :],
                            preferred_element_type=F32)
    return acc


def _mlp_gla_in_kernel(h_ref, gf_ref, wup_ref, wdn_ref, gm_ref, wq_ref, wk_ref, wv_ref, wg_ref,
                       wl_ref, wgk2_ref, bgk2_ref, h_out, q_out, k_out, v_out, og_out, gk_out):
    h2 = _mlp(h_ref[...], gf_ref[...], wup_ref, wdn_ref)
    h_out[...] = h2
    hn = _rms(h2, gm_ref[...]).astype(BF16)
    q_out[...] = jnp.dot(hn, wq_ref[...], preferred_element_type=F32).astype(q_out.dtype)
    k_out[...] = jnp.dot(hn, wk_ref[...], preferred_element_type=F32).astype(k_out.dtype)
    v_out[...] = jnp.dot(hn, wv_ref[...], preferred_element_type=F32).astype(v_out.dtype)
    og_out[...] = jnp.dot(hn, wg_ref[...], preferred_element_type=F32).astype(og_out.dtype)
    low = jnp.dot(hn, wl_ref[...], preferred_element_type=F32)
    z = _dot_split(low, wgk2_ref[...]) + bgk2_ref[...]
    gk_out[...] = jax.nn.log_sigmoid(z) * (1.0 / GLA_GATE_NORMALIZER)


def _mlp_gla_in(h1, gf, wup, wdn, gm, wq, wk, wv, wg, wl, wgk2, bgk2, tm=512):
    m, d = h1.shape
    dqk = wq.shape[1]
    row = lambda w: pl.BlockSpec((tm, w), lambda i: (i, 0))
    consts = (gf, wup, wdn, gm, wq, wk, wv, wg, wl, wgk2, bgk2)
    return pl.pallas_call(
        _mlp_gla_in_kernel,
        grid=(m // tm,),
        in_specs=[row(d)] + [_const_spec(c.shape) for c in consts],
        out_specs=[row(d), row(dqk), row(dqk), row(d), row(d), row(dqk)],
        out_shape=[jax.ShapeDtypeStruct((m, d), F32),
                   jax.ShapeDtypeStruct((m, dqk), BF16),
                   jax.ShapeDtypeStruct((m, dqk), BF16),
                   jax.ShapeDtypeStruct((m, d), BF16),
                   jax.ShapeDtypeStruct((m, d), BF16),
                   jax.ShapeDtypeStruct((m, dqk), F32)],
        compiler_params=_params(("arbitrary",)),
        name="mlp_gla_in",
    )(h1, *consts)


def _gla_scan_kernel(q_ref, k_ref, v_ref, g_ref, gn_ref, o_ref, s_ref):
    c = pl.program_id(1)

    @pl.when(c == 0)
    def _():
        s_ref[...] = jnp.zeros_like(s_ref)

    L = g_ref.shape[0]
    dk = g_ref.shape[1] // GLA_HEADS
    dv = v_ref.shape[1] // GLA_HEADS
    half, quarter = L // 2, L // 4
    row = lax.broadcasted_iota(jnp.int32, (L, L), 0)
    col = lax.broadcasted_iota(jnp.int32, (L, L), 1)
    tri = jnp.where(row >= col, 1.0, 0.0).astype(BF16)
    rblk, cblk = row // GLA_SUB, col // GLA_SUB
    diag_off = jnp.where(rblk == cblk, row - col, -1)
    mask_a = (row // half - col // half) == 1
    mask_b = jnp.where(rblk % 2 == 1, rblk - cblk, 0) == 1
    rowv = lax.broadcasted_iota(jnp.int32, (L, 1), 0)

    cum = _cumsum_rows(g_ref[...], tri)
    last = cum[L - 1:L, :]
    ref_a = cum[half - 1:half, :]
    ref_b = jnp.where(rowv < half, cum[quarter - 1:quarter, :], cum[half + quarter - 1:half + quarter, :])
    q_all = q_ref[...].astype(F32) * (dk ** -0.5)
    k_all = k_ref[...].astype(F32)
    gn = gn_ref[...]

    for h in range(GLA_HEADS):
        sk = slice(h * dk, (h + 1) * dk)
        sv = slice(h * dv, (h + 1) * dv)
        q, k, g = q_all[:, sk], k_all[:, sk], cum[:, sk]
        v = v_ref[:, sv]
        st = s_ref[h]
        inter = _dot_nt(q * jnp.exp(g), st)
        k_dec = k * jnp.exp(last[:, sk] - g)
        s_ref[h] = st * jnp.exp(last[:, sk]) + _dot_tn(v, k_dec)

        ra, rb = ref_a[:, sk], ref_b[:, sk]
        s_a = _dot_nt(q * jnp.exp(jnp.minimum(g - ra, 0.0)), k * jnp.exp(jnp.minimum(ra - g, 0.0)))
        s_b = _dot_nt(q * jnp.exp(jnp.minimum(g - rb, 0.0)), k * jnp.exp(jnp.minimum(rb - g, 0.0)))
        s_d = jnp.zeros((L, L), F32)
        for dlt in range(GLA_SUB):
            if dlt == 0:
                kr, gr = k, g
            else:
                kr, gr = pltpu.roll(k, dlt, axis=0), pltpu.roll(g, dlt, axis=0)
            e = jnp.exp(jnp.minimum(g - gr, 0.0))
            dsum = jnp.sum(q * kr * e, axis=-1, keepdims=True)
            s_d = jnp.where(diag_off == dlt, dsum, s_d)
        scores = jnp.where(mask_a, s_a, jnp.where(mask_b, s_b, s_d))
        o = inter + _dot(scores, v)
        o = o * lax.rsqrt(jnp.mean(o * o, axis=-1, keepdims=True) + NORM_EPS) * gn
        o_ref[:, sv] = o.astype(o_ref.dtype)


def _gla_scan(q, k, v, gk, gn, batch, seq):
    m, dqk = q.shape
    d = v.shape[1]
    nc = seq // CHUNK
    idx = lambda bi, ci: (bi * nc + ci, 0)
    return pl.pallas_call(
        _gla_scan_kernel,
        grid=(batch, nc),
        in_specs=[pl.BlockSpec((CHUNK, dqk), idx), pl.BlockSpec((CHUNK, dqk), idx),
                  pl.BlockSpec((CHUNK, d), idx), pl.BlockSpec((CHUNK, dqk), idx),
                  _const_spec(gn.shape)],
        out_specs=pl.BlockSpec((CHUNK, d), idx),
        out_shape=jax.ShapeDtypeStruct((m, d), BF16),
        scratch_shapes=[pltpu.VMEM((GLA_HEADS, d // GLA_HEADS, dqk // GLA_HEADS), F32)],
        compiler_params=_params(("arbitrary", "arbitrary")),
        name="gla_scan",
    )(q, k, v, gk, gn)


def _gla_post_kernel(o_ref, og_ref, h_ref, wo_ref, gf_ref, wup_ref, wdn_ref, fg_ref, out_ref):
    og = og_ref[...].astype(F32)
    gated = o_ref[...].astype(F32) * (og * jax.nn.sigmoid(og))
    h3 = h_ref[...] + _dot(gated, wo_ref[...])
    h4 = _mlp(h3, gf_ref[...], wup_ref, wdn_ref)
    out_ref[...] = _rms(h4, fg_ref[...])


def _gla_post(o, og, h2, wo, gf, wup, wdn, fg, tm=512):
    m, d = h2.shape
    row_spec = pl.BlockSpec((tm, d), lambda i: (i, 0))
    consts = (wo, gf, wup, wdn, fg)
    return pl.pallas_call(
        _gla_post_kernel,
        grid=(m // tm,),
        in_specs=[row_spec] * 3 + [_const_spec(c.shape) for c in consts],
        out_specs=row_spec,
        out_shape=jax.ShapeDtypeStruct((m, d), F32),
        compiler_params=_params(("arbitrary",)),
        name="gla_post",
    )(o, og, h2, *consts)


def kernel(x, norm_mix_g, norm_ffn_g, mlp_up, mlp_down, rwkv_mu, rwkv_w_rkv, rwkv_w0, rwkv_w1, rwkv_w2, rwkv_a0, rwkv_a1, rwkv_a2, rwkv_g1, rwkv_g2, rwkv_k_k, rwkv_k_a, rwkv_r_k, rwkv_lnx_w, rwkv_lnx_b, rwkv_w_o, gla_w_in, gla_w_gk2, gla_b_gk2, gla_gnorm_g, gla_w_o, final_g):
    batch, seq, d = x.shape
    assert seq % CHUNK == 0 and d % RWKV_HEAD == 0
    m = batch * seq
    heads = d // RWKV_HEAD
    x2 = x.reshape(m, d)
    row = lambda t: t.reshape(1, -1).astype(F32)
    bf = lambda t: t.astype(BF16)
    head_of = jnp.arange(d, dtype=jnp.int32) // RWKV_HEAD
    e = (head_of[:, None] == jnp.arange(heads, dtype=jnp.int32)[None, :]).astype(BF16)
    et = e.T

    r, lw, kmod, v, a, b, gate = _rwkv_pre(
        x2, seq, row(norm_mix_g[0]), rwkv_mu[0], bf(rwkv_w_rkv[0]), row(rwkv_w0[0]),
        bf(rwkv_w1[0]), bf(rwkv_w2[0]), row(rwkv_a0[0]), bf(rwkv_a1[0]), bf(rwkv_a2[0]),
        bf(rwkv_g1[0]), bf(rwkv_g2[0]), row(rwkv_k_k[0]), row(rwkv_k_a[0]), e, et)
    y = _rwkv_scan(r, lw, kmod, v, a, b, batch, seq)
    h1 = _rwkv_post(y, r, kmod, v, gate, x2, row(rwkv_lnx_w[0]), row(rwkv_lnx_b[0]),
                    row(rwkv_r_k[0]), bf(rwkv_w_o[0]), e, et)

    w_in = gla_w_in[0]
    dqk = gla_w_gk2.shape[2]
    o_q, o_k, o_v, o_g = 0, dqk, 2 * dqk, 2 * dqk + d
    h2, q, k, vv, og, gk = _mlp_gla_in(
        h1, row(norm_ffn_g[0]), bf(mlp_up[0]), bf(mlp_down[0]), row(norm_mix_g[1]),
        bf(w_in[:, o_q:o_k]), bf(w_in[:, o_k:o_v]), bf(w_in[:, o_v:o_g]),
        bf(w_in[:, o_g:o_g + d]), bf(w_in[:, o_g + d:]), gla_w_gk2[0].astype(F32),
        row(gla_b_gk2[0]))
    o = _gla_scan(q, k, vv, gk, row(gla_gnorm_g[0]), batch, seq)
    out = _gla_post(o, og, h2, bf(gla_w_o[0]), row(norm_ffn_g[1]), bf(mlp_up[1]),
                    bf(mlp_down[1]), row(final_g))
    return out.reshape(batch, seq, d)
```

```python
import functools
import math

import jax
import jax.numpy as jnp
from jax import lax
from jax.experimental import pallas as pl
from jax.experimental.pallas import tpu as pltpu

F32 = jnp.float32
BF16 = jnp.bfloat16

CHUNK = 64
RWKV_HEAD = 64
GLA_HEADS = 4
GLA_GATE_NORMALIZER = 16.0
GLA_SUB = 8
RWKV_HEAD_GROUP = 16
NORM_EPS = 1e-5
GN_EPS = 1e-5 * RWKV_HEAD
V7X_VMEM_LIMIT_BYTES = 56 * 1024 * 1024


def _dot(a, b):
    return jnp.dot(a.astype(BF16), b.astype(BF16), preferred_element_type=F32)


def _dot_nt(a, b):
    return lax.dot_general(a.astype(BF16), b.astype(BF16), (((1,), (1,)), ((), ())),
                           preferred_element_type=F32)


def _dot_tn(a, b):
    return lax.dot_general(a.astype(BF16), b.astype(BF16), (((0,), (0,)), ((), ())),
                           preferred_element_type=F32)


def _dot_split(a, b):
    hi = a.astype(BF16)
    lo = (a - hi.astype(F32)).astype(BF16)
    bb = b.astype(BF16)
    return (jnp.dot(hi, bb, preferred_element_type=F32)
            + jnp.dot(lo, bb, preferred_element_type=F32))


def _rms(x, g):
    return x * lax.rsqrt(jnp.mean(x * x, axis=-1, keepdims=True) + NORM_EPS) * g


def _cumsum_rows(x, tri):
    hi = x.astype(BF16)
    lo = (x - hi.astype(F32)).astype(BF16)
    return (jnp.dot(tri, hi, preferred_element_type=F32)
            + jnp.dot(tri, lo, preferred_element_type=F32))


def _const_spec(shape):
    n = len(shape)
    return pl.BlockSpec(shape, lambda *_: (0,) * n)


def _params(sem):
    return pltpu.CompilerParams(dimension_semantics=sem,
                                vmem_limit_bytes=V7X_VMEM_LIMIT_BYTES)


def _rwkv_pre_kernel(x_ref, xh_ref, g_ref, mu_ref, wrkv_ref, w0_ref, w1_ref, w2_ref,
                     a0_ref, a1_ref, a2_ref, g1_ref, g2_ref, kk_ref, ka_ref, e_ref, et_ref,
                     r_out, lw_out, k_out, v_out, a_out, b_out, gate_out, *, tiles_per_seq):
    i = pl.program_id(0)
    g = g_ref[...]
    x = x_ref[...]
    tm = x.shape[0]
    hn = _rms(x, g)
    hp = _rms(xh_ref[...], g)[7:8, :]
    hp = jnp.where(i % tiles_per_seq == 0, 0.0, hp)
    row = lax.broadcasted_iota(jnp.int32, (tm, 1), 0)
    sh = jnp.where(row == 0, hp, pltpu.roll(hn, 1, axis=0))
    xx = sh - hn
    mu = mu_ref[...]

    def mix(n):
        return (hn + xx * mu[n:n + 1, :]).astype(BF16)

    r = jnp.dot(mix(0), wrkv_ref[0], preferred_element_type=F32)
    k = jnp.dot(mix(1), wrkv_ref[1], preferred_element_type=F32)
    v = jnp.dot(mix(2), wrkv_ref[2], preferred_element_type=F32)
    zw = w0_ref[...] + _dot(jnp.tanh(_dot(mix(3), w1_ref[...])), w2_ref[...])
    lw = -math.exp(-0.5) * jax.nn.sigmoid(zw)
    asig = jax.nn.sigmoid(a0_ref[...] + _dot(_dot(mix(4), a1_ref[...]), a2_ref[...]))
    gate = _dot(jax.nn.sigmoid(_dot(mix(5), g1_ref[...])), g2_ref[...])

    kk = k * kk_ref[...]
    ss = _dot(kk * kk, e_ref[...])
    inv = 1.0 / jnp.maximum(jnp.sqrt(ss), 1e-12)
    kkn = kk * _dot_split(inv, et_ref[...])
    kmod = k * (1.0 + (asig - 1.0) * ka_ref[...])

    r_out[...] = r.astype(r_out.dtype)
    lw_out[...] = lw
    k_out[...] = kmod.astype(k_out.dtype)
    v_out[...] = v.astype(v_out.dtype)
    a_out[...] = (-kkn).astype(a_out.dtype)
    b_out[...] = (kkn * asig).astype(b_out.dtype)
    gate_out[...] = gate.astype(gate_out.dtype)


def _rwkv_pre(x2, seq, g, mu, wrkv, w0, w1, w2, a0, a1, a2, g1, g2, k_k, k_a, e, et, tm=512):
    m, d = x2.shape
    tiles_per_seq = seq // tm
    row_spec = pl.BlockSpec((tm, d), lambda i: (i, 0))
    halo_spec = pl.BlockSpec((8, d), lambda i: (jnp.maximum(i * (tm // 8) - 1, 0), 0))
    consts = (g, mu, wrkv, w0, w1, w2, a0, a1, a2, g1, g2, k_k, k_a, e, et)
    out_bf = jax.ShapeDtypeStruct((m, d), BF16)
    out_f32 = jax.ShapeDtypeStruct((m, d), F32)
    return pl.pallas_call(
        functools.partial(_rwkv_pre_kernel, tiles_per_seq=tiles_per_seq),
        grid=(m // tm,),
        in_specs=[row_spec, halo_spec] + [_const_spec(c.shape) for c in consts],
        out_specs=[row_spec] * 7,
        out_shape=[out_bf, out_f32, out_bf, out_bf, out_bf, out_bf, out_bf],
        compiler_params=_params(("arbitrary",)),
        name="rwkv_pre",
    )(x2, x2, *consts)


def _rwkv_scan_kernel(r_ref, lw_ref, k_ref, v_ref, a_ref, b_ref, y_ref, h_ref):
    c = pl.program_id(1)

    @pl.when(c == 0)
    def _():
        h_ref[...] = jnp.zeros_like(h_ref)

    L = lw_ref.shape[0]
    n = RWKV_HEAD
    heads = lw_ref.shape[1] // n
    row = lax.broadcasted_iota(jnp.int32, (L, L), 0)
    col = lax.broadcasted_iota(jnp.int32, (L, L), 1)
    strict = row > col
    incl = row >= col
    eye = row == col
    tri = jnp.where(incl, 1.0, 0.0).astype(BF16)

    lw = lw_ref[...]
    cl = _cumsum_rows(lw, tri)
    cl_last = cl[L - 1:L, :]
    p = jnp.exp(cl)
    ip = jnp.exp(-cl)
    pprev = jnp.exp(cl - lw)
    pl_rel = jnp.exp(cl_last - cl)
    p_last = jnp.exp(cl_last)
    r = r_ref[...].astype(F32)
    k = k_ref[...].astype(F32)
    a = a_ref[...].astype(F32)
    b = b_ref[...].astype(F32)
    v_all = v_ref[...]
    rt_all = r * p
    at_all = a * pprev
    bt_all = b * ip
    kt_all = k * ip
    bh_all = b * pl_rel
    kh_all = k * pl_rel

    row4 = lax.broadcasted_iota(jnp.int32, (2 * L, 2 * L), 0)
    col4 = lax.broadcasted_iota(jnp.int32, (2 * L, 2 * L), 1)
    mask4 = jnp.where(row4 < L, row4, row4 - L + 1) > col4 % L
    zeros_ln = jnp.zeros((L, n), F32)
    steps = int(math.log2(L))

    ys = [None] * heads
    for g0 in range(0, heads, RWKV_HEAD_GROUP):
        hs_ids = list(range(g0, min(g0 + RWKV_HEAD_GROUP, heads)))
        sls = {h: slice(h * n, (h + 1) * n) for h in hs_ids}
        s1 = {h: jnp.where(mask4, _dot_nt(
            jnp.concatenate([at_all[:, sls[h]], rt_all[:, sls[h]]], axis=0),
            jnp.concatenate([bt_all[:, sls[h]], kt_all[:, sls[h]]], axis=0)), 0.0) for h in hs_ids}
        sb = {h: s1[h][:, :n] for h in hs_ids}
        r2 = {h: _dot(jnp.concatenate([s1[h], jnp.concatenate([zeros_ln, kh_all[:, sls[h]]], axis=0).T],
                                      axis=0),
                      jnp.concatenate([zeros_ln.astype(v_all.dtype), v_all[:, sls[h]]], axis=0))
              for h in hs_ids}
        z = {h: jnp.concatenate([at_all[:, sls[h]], r2[h][:L]], axis=1) for h in hs_ids}
        pw = {h: sb[h][:L] for h in hs_ids}
        for it in range(steps):
            if it < steps - 1:
                w = {h: _dot(pw[h], jnp.concatenate([z[h], pw[h]], axis=1)) for h in hs_ids}
                z = {h: z[h] + w[h][:, :2 * n] for h in hs_ids}
                pw = {h: w[h][:, 2 * n:] for h in hs_ids}
            else:
                z = {h: z[h] + _dot(pw[h], z[h]) for h in hs_ids}
        r9 = {h: _dot(jnp.concatenate([sb[h][L:], bh_all[:, sls[h]].T], axis=0), z[h]) for h in hs_ids}
        lhs10 = {}
        for h in hs_ids:
            rh = rt_all[:, sls[h]] + r9[h][:L, :n]
            gm = r9[h][L:, :n] + jnp.where(eye, p_last[:, sls[h]], 0.0)
            lhs10[h] = jnp.concatenate([rh, gm], axis=0)
        r10 = {h: _dot(lhs10[h], h_ref[h]) for h in hs_ids}
        for h in hs_ids:
            ys[h] = r10[h][:L] + r9[h][:L, n:] + r2[h][L:2 * L]
            h_ref[h] = r10[h][L:] + r9[h][L:, n:] + r2[h][2 * L:]
    y_ref[...] = jnp.concatenate(ys, axis=1)


def _rwkv_scan(r, lw, k, v, a, b, batch, seq):
    m, d = lw.shape
    nc = seq // CHUNK
    spec = pl.BlockSpec((CHUNK, d), lambda bi, ci: (bi * nc + ci, 0))
    heads = d // RWKV_HEAD
    return pl.pallas_call(
        _rwkv_scan_kernel,
        grid=(batch, nc),
        in_specs=[spec] * 6,
        out_specs=spec,
        out_shape=jax.ShapeDtypeStruct((m, d), F32),
        scratch_shapes=[pltpu.VMEM((heads, RWKV_HEAD, RWKV_HEAD), F32)],
        compiler_params=_params(("arbitrary", "arbitrary")),
        name="rwkv_scan",
    )(r, lw, k, v, a, b)


def _rwkv_post_kernel(y_ref, r_ref, k_ref, v_ref, gate_ref, x_ref, lnw_ref, lnb_ref, rk_ref,
                      wo_ref, e_ref, et_ref, h_out):
    e = e_ref[...]
    et = et_ref[...]
    inv_n = 1.0 / RWKV_HEAD
    y = y_ref[...]
    mean = _dot_split(_dot(y, e) * inv_n, et)
    yc = y - mean
    var = _dot(yc * yc, e) * inv_n
    yn = yc * _dot_split(lax.rsqrt(var + GN_EPS), et) * lnw_ref[...] + lnb_ref[...]
    r = r_ref[...].astype(F32)
    k = k_ref[...].astype(F32)
    bonus = _dot_split(_dot(r * k * rk_ref[...], e), et) * v_ref[...].astype(F32)
    out = (yn + bonus) * gate_ref[...].astype(F32)
    h_out[...] = x_ref[...] + _dot(out, wo_ref[...])


def _rwkv_post(y, r, k, v, gate, x2, lnw, lnb, rk, wo, e, et, tm=512):
    m, d = y.shape
    row_spec = pl.BlockSpec((tm, d), lambda i: (i, 0))
    consts = (lnw, lnb, rk, wo, e, et)
    return pl.pallas_call(
        _rwkv_post_kernel,
        grid=(m // tm,),
        in_specs=[row_spec] * 6 + [_const_spec(c.shape) for c in consts],
        out_specs=row_spec,
        out_shape=jax.ShapeDtypeStruct((m, d), F32),
        compiler_params=_params(("arbitrary",)),
        name="rwkv_post",
    )(y, r, k, v, gate, x2, *consts)


def _mlp(h, g, wup_ref, wdn_ref):
    d = h.shape[1]
    xn = _rms(h, g).astype(BF16)
    acc = h
    for j in range(wup_ref.shape[1] // d):
        up = jnp.dot(xn, wup_ref[:, j * d:(j + 1) * d], preferred_element_type=F32)
        up = jnp.maximum(up, 0.0)
        acc = acc + jnp.dot((up * up).astype(BF16), wdn_ref[j * d:(j + 1) * d, :],
                            preferred_element_type=F32)
    return acc


def _mlp_gla_in_kernel(h_ref, gf_ref, wup_ref, wdn_ref, gm_ref, wq_ref, wk_ref, wv_ref, wg_ref,
                       wl_ref, wgk2_ref, bgk2_ref, h_out, q_out, k_out, v_out, og_out, gk_out):
    h2 = _mlp(h_ref[...], gf_ref[...], wup_ref, wdn_ref)
    h_out[...] = h2
    hn = _rms(h2, gm_ref[...]).astype(BF16)
    q_out[...] = jnp.dot(hn, wq_ref[...], preferred_element_type=F32).astype(q_out.dtype)
    k_out[...] = jnp.dot(hn, wk_ref[...], preferred_element_type=F32).astype(k_out.dtype)
    v_out[...] = jnp.dot(hn, wv_ref[...], preferred_element_type=F32).astype(v_out.dtype)
    og_out[...] = jnp.dot(hn, wg_ref[...], preferred_element_type=F32).astype(og_out.dtype)
    low = jnp.dot(hn, wl_ref[...], preferred_element_type=F32)
    z = _dot_split(low, wgk2_ref[...]) + bgk2_ref[...]
    gk_out[...] = jax.nn.log_sigmoid(z) * (1.0 / GLA_GATE_NORMALIZER)


def _mlp_gla_in(h1, gf, wup, wdn, gm, wq, wk, wv, wg, wl, wgk2, bgk2, tm=512):
    m, d = h1.shape
    dqk = wq.shape[1]
    row = lambda w: pl.BlockSpec((tm, w), lambda i: (i, 0))
    consts = (gf, wup, wdn, gm, wq, wk, wv, wg, wl, wgk2, bgk2)
    return pl.pallas_call(
        _mlp_gla_in_kernel,
        grid=(m // tm,),
        in_specs=[row(d)] + [_const_spec(c.shape) for c in consts],
        out_specs=[row(d), row(dqk), row(dqk), row(d), row(d), row(dqk)],
        out_shape=[jax.ShapeDtypeStruct((m, d), F32),
                   jax.ShapeDtypeStruct((m, dqk), BF16),
                   jax.ShapeDtypeStruct((m, dqk), BF16),
                   jax.ShapeDtypeStruct((m, d), BF16),
                   jax.ShapeDtypeStruct((m, d), BF16),
                   jax.ShapeDtypeStruct((m, dqk), F32)],
        compiler_params=_params(("arbitrary",)),
        name="mlp_gla_in",
    )(h1, *consts)


def _gla_scan_kernel(q_ref, k_ref, v_ref, g_ref, gn_ref, o_ref, s_ref, cum_ref, kf_ref):
    c = pl.program_id(1)

    @pl.when(c == 0)
    def _():
        s_ref[...] = jnp.zeros_like(s_ref)

    L = g_ref.shape[0]
    dk = g_ref.shape[1] // GLA_HEADS
    dv = v_ref.shape[1] // GLA_HEADS
    sub = GLA_SUB
    row = lax.broadcasted_iota(jnp.int32, (L, L), 0)
    col = lax.broadcasted_iota(jnp.int32, (L, L), 1)
    tri = jnp.where(row >= col, 1.0, 0.0).astype(BF16)
    diag_col = jnp.where(row // sub == col // sub, jnp.where(col <= row, col % sub, -1), -1)
    eye_k = (lax.broadcasted_iota(jnp.int32, (dk, dk), 0)
             == lax.broadcasted_iota(jnp.int32, (dk, dk), 1))

    cum = _cumsum_rows(g_ref[...], tri) * math.log2(math.e)
    cum_ref[...] = cum
    kf_ref[...] = k_ref[...].astype(F32)
    last = cum[L - 1:L, :]
    q_all = q_ref[...].astype(F32) * (dk ** -0.5)
    k_all = kf_ref[...]
    gn = gn_ref[...]

    levels = []
    size = L // 2
    while size >= sub:
        nblk = L // (2 * size)
        ref = jnp.concatenate(
            [jnp.broadcast_to(cum[(2 * b + 1) * size - 1:(2 * b + 1) * size, :], (2 * size, cum.shape[1]))
             for b in range(nblk)], axis=0)
        rb, cb = row // size, col // size
        levels.append((ref, jnp.where(rb % 2 == 1, rb - cb, 0) == 1))
        size //= 2

    for h in range(GLA_HEADS):
        sk = slice(h * dk, (h + 1) * dk)
        sv = slice(h * dv, (h + 1) * dv)
        q, k, g = q_all[:, sk], k_all[:, sk], cum[:, sk]
        scores = jnp.zeros((L, L), F32)
        for j in range(sub):
            kb = jnp.concatenate([jnp.broadcast_to(kf_ref[b * sub + j:b * sub + j + 1, sk], (sub, dk))
                                  for b in range(L // sub)], axis=0)
            gb = jnp.concatenate([jnp.broadcast_to(cum_ref[b * sub + j:b * sub + j + 1, sk], (sub, dk))
                                  for b in range(L // sub)], axis=0)
            e = jnp.exp2(jnp.minimum(g - gb, 0.0))
            dsum = jnp.sum(q * kb * e, axis=-1, keepdims=True)
            scores = jnp.where(diag_col == j, dsum, scores)
        for ref, mask in levels:
            rr = ref[:, sk]
            s_l = _dot_nt(q * jnp.exp2(jnp.minimum(g - rr, 0.0)), k * jnp.exp2(jnp.minimum(rr - g, 0.0)))
            scores = jnp.where(mask, s_l, scores)

        q_in = q * jnp.exp2(g)
        k_dec = k * jnp.exp2(last[:, sk] - g)
        decay = jnp.where(eye_k, jnp.exp2(last[:, sk]), 0.0)
        lhs = jnp.concatenate(
            [jnp.concatenate([q_in, scores], axis=1), jnp.concatenate([decay, k_dec.T], axis=1)],
            axis=0).astype(BF16)
        rhs = jnp.concatenate([s_ref[h].astype(BF16), v_ref[:, sv]], axis=0)
        res = jnp.dot(lhs, rhs, preferred_element_type=F32)
        s_ref[h] = res[L:]
        o = res[:L]
        o = o * lax.rsqrt(jnp.mean(o * o, axis=-1, keepdims=True) + NORM_EPS) * gn
        o_ref[:, sv] = o.astype(o_ref.dtype)


def _gla_scan(q, k, v, gk, gn, batch, seq):
    m, dqk = q.shape
    d = v.shape[1]
    nc = seq // CHUNK
    idx = lambda bi, ci: (bi * nc + ci, 0)
    return pl.pallas_call(
        _gla_scan_kernel,
        grid=(batch, nc),
        in_specs=[pl.BlockSpec((CHUNK, dqk), idx), pl.BlockSpec((CHUNK, dqk), idx),
                  pl.BlockSpec((CHUNK, d), idx), pl.BlockSpec((CHUNK, dqk), idx),
                  _const_spec(gn.shape)],
        out_specs=pl.BlockSpec((CHUNK, d), idx),
        out_shape=jax.ShapeDtypeStruct((m, d), BF16),
        scratch_shapes=[pltpu.VMEM((GLA_HEADS, dqk // GLA_HEADS, d // GLA_HEADS), F32),
                        pltpu.VMEM((CHUNK, dqk), F32), pltpu.VMEM((CHUNK, dqk), F32)],
        compiler_params=_params(("arbitrary", "arbitrary")),
        name="gla_scan",
    )(q, k, v, gk, gn)


def _gla_post_kernel(o_ref, og_ref, h_ref, wo_ref, gf_ref, wup_ref, wdn_ref, fg_ref, out_ref):
    og = og_ref[...].astype(F32)
    gated = o_ref[...].astype(F32) * (og * jax.nn.sigmoid(og))
    h3 = h_ref[...] + _dot(gated, wo_ref[...])
    h4 = _mlp(h3, gf_ref[...], wup_ref, wdn_ref)
    out_ref[...] = _rms(h4, fg_ref[...])


def _gla_post(o, og, h2, wo, gf, wup, wdn, fg, tm=512):
    m, d = h2.shape
    row_spec = pl.BlockSpec((tm, d), lambda i: (i, 0))
    consts = (wo, gf, wup, wdn, fg)
    return pl.pallas_call(
        _gla_post_kernel,
        grid=(m // tm,),
        in_specs=[row_spec] * 3 + [_const_spec(c.shape) for c in consts],
        out_specs=row_spec,
        out_shape=jax.ShapeDtypeStruct((m, d), F32),
        compiler_params=_params(("arbitrary",)),
        name="gla_post",
    )(o, og, h2, *consts)


def kernel(x, norm_mix_g, norm_ffn_g, mlp_up, mlp_down, rwkv_mu, rwkv_w_rkv, rwkv_w0, rwkv_w1, rwkv_w2, rwkv_a0, rwkv_a1, rwkv_a2, rwkv_g1, rwkv_g2, rwkv_k_k, rwkv_k_a, rwkv_r_k, rwkv_lnx_w, rwkv_lnx_b, rwkv_w_o, gla_w_in, gla_w_gk2, gla_b_gk2, gla_gnorm_g, gla_w_o, final_g):
    batch, seq, d = x.shape
    assert seq % CHUNK == 0 and d % RWKV_HEAD == 0
    m = batch * seq
    heads = d // RWKV_HEAD
    x2 = x.reshape(m, d)
    row = lambda t: t.reshape(1, -1).astype(F32)
    bf = lambda t: t.astype(BF16)
    head_of = jnp.arange(d, dtype=jnp.int32) // RWKV_HEAD
    e = (head_of[:, None] == jnp.arange(heads, dtype=jnp.int32)[None, :]).astype(BF16)
    et = e.T

    r, lw, kmod, v, a, b, gate = _rwkv_pre(
        x2, seq, row(norm_mix_g[0]), rwkv_mu[0], bf(rwkv_w_rkv[0]), row(rwkv_w0[0]),
        bf(rwkv_w1[0]), bf(rwkv_w2[0]), row(rwkv_a0[0]), bf(rwkv_a1[0]), bf(rwkv_a2[0]),
        bf(rwkv_g1[0]), bf(rwkv_g2[0]), row(rwkv_k_k[0]), row(rwkv_k_a[0]), e, et)
    y = _rwkv_scan(r, lw, kmod, v, a, b, batch, seq)
    h1 = _rwkv_post(y, r, kmod, v, gate, x2, row(rwkv_lnx_w[0]), row(rwkv_lnx_b[0]),
                    row(rwkv_r_k[0]), bf(rwkv_w_o[0]), e, et)

    w_in = gla_w_in[0]
    dqk = gla_w_gk2.shape[2]
    o_q, o_k, o_v, o_g = 0, dqk, 2 * dqk, 2 * dqk + d
    h2, q, k, vv, og, gk = _mlp_gla_in(
        h1, row(norm_ffn_g[0]), bf(mlp_up[0]), bf(mlp_down[0]), row(norm_mix_g[1]),
        bf(w_in[:, o_q:o_k]), bf(w_in[:, o_k:o_v]), bf(w_in[:, o_v:o_g]),
        bf(w_in[:, o_g:o_g + d]), bf(w_in[:, o_g + d:]), gla_w_gk2[0].astype(F32),
        row(gla_b_gk2[0]))
    o = _gla_scan(q, k, vv, gk, row(gla_gnorm_g[0]), batch, seq)
    out = _gla_post(o, og, h2, bf(gla_w_o[0]), row(norm_ffn_g[1]), bf(mlp_up[1]),
                    bf(mlp_down[1]), row(final_g))
    return out.reshape(batch, seq, d)
```

```python
import functools
import math

import jax
import jax.numpy as jnp
from jax import lax
from jax.experimental import pallas as pl
from jax.experimental.pallas import tpu as pltpu

F32 = jnp.float32
BF16 = jnp.bfloat16

CHUNK = 64
RWKV_HEAD = 64
GLA_HEADS = 4
GLA_GATE_NORMALIZER = 16.0
GLA_SUB = 8
NORM_EPS = 1e-5
GN_EPS = 1e-5 * RWKV_HEAD
V7X_VMEM_LIMIT_BYTES = 56 * 1024 * 1024


def _dot(a, b):
    return jnp.dot(a.astype(BF16), b.astype(BF16), preferred_element_type=F32)


def _dot_nt(a, b):
    return lax.dot_general(a.astype(BF16), b.astype(BF16), (((1,), (1,)), ((), ())),
                           preferred_element_type=F32)


def _dot_tn(a, b):
    return lax.dot_general(a.astype(BF16), b.astype(BF16), (((0,), (0,)), ((), ())),
                           preferred_element_type=F32)


def _dot_split(a, b):
    hi = a.astype(BF16)
    lo = (a - hi.astype(F32)).astype(BF16)
    bb = b.astype(BF16)
    return (jnp.dot(hi, bb, preferred_element_type=F32)
            + jnp.dot(lo, bb, preferred_element_type=F32))


def _rms(x, g):
    return x * lax.rsqrt(jnp.mean(x * x, axis=-1, keepdims=True) + NORM_EPS) * g


def _cumsum_rows(x, tri):
    hi = x.astype(BF16)
    lo = (x - hi.astype(F32)).astype(BF16)
    return (jnp.dot(tri, hi, preferred_element_type=F32)
            + jnp.dot(tri, lo, preferred_element_type=F32))


def _const_spec(shape):
    n = len(shape)
    return pl.BlockSpec(shape, lambda *_: (0,) * n)


def _params(sem):
    return pltpu.CompilerParams(dimension_semantics=sem,
                                vmem_limit_bytes=V7X_VMEM_LIMIT_BYTES)


def _rwkv_pre_kernel(x_ref, xh_ref, g_ref, mu_ref, wrkv_ref, w0_ref, w1_ref, w2_ref,
                     a0_ref, a1_ref, a2_ref, g1_ref, g2_ref, kk_ref, ka_ref, e_ref, et_ref,
                     r_out, lw_out, k_out, v_out, a_out, b_out, gate_out, *, tiles_per_seq):
    i = pl.program_id(0)
    g = g_ref[...]
    x = x_ref[...]
    tm = x.shape[0]
    hn = _rms(x, g)
    hp = _rms(xh_ref[...], g)[7:8, :]
    hp = jnp.where(i % tiles_per_seq == 0, 0.0, hp)
    row = lax.broadcasted_iota(jnp.int32, (tm, 1), 0)
    sh = jnp.where(row == 0, hp, pltpu.roll(hn, 1, axis=0))
    xx = sh - hn
    mu = mu_ref[...]

    def mix(n):
        return (hn + xx * mu[n:n + 1, :]).astype(BF16)

    r = jnp.dot(mix(0), wrkv_ref[0], preferred_element_type=F32)
    k = jnp.dot(mix(1), wrkv_ref[1], preferred_element_type=F32)
    v = jnp.dot(mix(2), wrkv_ref[2], preferred_element_type=F32)
    zw = w0_ref[...] + _dot(jnp.tanh(_dot(mix(3), w1_ref[...])), w2_ref[...])
    lw = -math.exp(-0.5) * jax.nn.sigmoid(zw)
    asig = jax.nn.sigmoid(a0_ref[...] + _dot(_dot(mix(4), a1_ref[...]), a2_ref[...]))
    gate = _dot(jax.nn.sigmoid(_dot(mix(5), g1_ref[...])), g2_ref[...])

    kk = k * kk_ref[...]
    ss = _dot(kk * kk, e_ref[...])
    inv = 1.0 / jnp.maximum(jnp.sqrt(ss), 1e-12)
    kkn = kk * _dot_split(inv, et_ref[...])
    kmod = k * (1.0 + (asig - 1.0) * ka_ref[...])

    r_out[...] = r.astype(r_out.dtype)
    lw_out[...] = lw
    k_out[...] = kmod.astype(k_out.dtype)
    v_out[...] = v.astype(v_out.dtype)
    a_out[...] = (-kkn).astype(a_out.dtype)
    b_out[...] = (kkn * asig).astype(b_out.dtype)
    gate_out[...] = gate.astype(gate_out.dtype)


def _rwkv_pre(x2, seq, g, mu, wrkv, w0, w1, w2, a0, a1, a2, g1, g2, k_k, k_a, e, et, tm=512):
    m, d = x2.shape
    tiles_per_seq = seq // tm
    row_spec = pl.BlockSpec((tm, d), lambda i: (i, 0))
    halo_spec = pl.BlockSpec((8, d), lambda i: (jnp.maximum(i * (tm // 8) - 1, 0), 0))
    consts = (g, mu, wrkv, w0, w1, w2, a0, a1, a2, g1, g2, k_k, k_a, e, et)
    out_bf = jax.ShapeDtypeStruct((m, d), BF16)
    out_f32 = jax.ShapeDtypeStruct((m, d), F32)
    return pl.pallas_call(
        functools.partial(_rwkv_pre_kernel, tiles_per_seq=tiles_per_seq),
        grid=(m // tm,),
        in_specs=[row_spec, halo_spec] + [_const_spec(c.shape) for c in consts],
        out_specs=[row_spec] * 7,
        out_shape=[out_bf, out_f32, out_bf, out_bf, out_bf, out_bf, out_bf],
        compiler_params=_params(("arbitrary",)),
        name="rwkv_pre",
    )(x2, x2, *consts)


def _rwkv_scan_kernel(r_ref, lw_ref, k_ref, v_ref, a_ref, b_ref, y_ref, h_ref):
    c = pl.program_id(1)

    @pl.when(c == 0)
    def _():
        h_ref[...] = jnp.zeros_like(h_ref)

    L = lw_ref.shape[0]
    n = RWKV_HEAD
    pw_ = 2 * n
    heads = lw_ref.shape[1] // n
    row = lax.broadcasted_iota(jnp.int32, (L, L), 0)
    col = lax.broadcasted_iota(jnp.int32, (L, L), 1)
    tri = jnp.where(row >= col, 1.0, 0.0).astype(BF16)

    lw = lw_ref[...]
    cl = _cumsum_rows(lw, tri)
    cl_last = cl[L - 1:L, :]
    p = jnp.exp(cl)
    ip = jnp.exp(-cl)
    pprev = jnp.exp(cl - lw)
    pl_rel = jnp.exp(cl_last - cl)
    p_last = jnp.exp(cl_last)
    r = r_ref[...].astype(F32)
    k = k_ref[...].astype(F32)
    a = a_ref[...].astype(F32)
    b = b_ref[...].astype(F32)
    rt_all = r * p
    at_all = a * pprev
    bt_all = (b * ip).astype(BF16)
    kt_all = (k * ip).astype(BF16)
    bh_all = b * pl_rel
    kh_all = k * pl_rel
    v_all = v_ref[...].astype(F32)

    lane = lax.broadcasted_iota(jnp.int32, (L, pw_), 1)
    own = [lane < n, lane >= n]
    row4 = lax.broadcasted_iota(jnp.int32, (2 * L, 2 * L), 0)
    col4 = lax.broadcasted_iota(jnp.int32, (2 * L, 2 * L), 1)
    mask4 = jnp.where(row4 < L, row4, row4 - L + 1) > col4 % L
    rowk = lax.broadcasted_iota(jnp.int32, (n, pw_), 0)
    lanek = lax.broadcasted_iota(jnp.int32, (n, pw_), 1)
    diag = [lanek == rowk, lanek == rowk + n]
    zeros_lp = jnp.zeros((L, pw_), F32)
    zeros_np = jnp.zeros((n, pw_), F32)
    steps = int(math.log2(L))

    pairs = heads // 2
    hs_ids = list(range(heads))
    psl = [slice(pi * pw_, (pi + 1) * pw_) for pi in range(pairs)]
    at_p = [at_all[:, s] for s in psl]
    rt_p = [rt_all[:, s] for s in psl]
    yt_p = [jnp.concatenate([bt_all[:, s], kt_all[:, s]], axis=0) for s in psl]
    bht_p = [bh_all[:, s].T for s in psl]
    kht_p = [jnp.concatenate([zeros_lp, kh_all[:, s]], axis=0).T for s in psl]
    rhs2_p = [jnp.concatenate([zeros_lp, pltpu.roll(v_all[:, s], n, axis=1)], axis=0).astype(BF16)
              for s in psl]

    def sub(h):
        return slice((h % 2) * n, (h % 2 + 1) * n)

    s1 = {h: jnp.where(mask4, _dot_nt(
        jnp.concatenate([jnp.where(own[h % 2], at_p[h // 2], 0.0),
                         jnp.where(own[h % 2], rt_p[h // 2], 0.0)], axis=0), yt_p[h // 2]), 0.0)
          for h in hs_ids}
    r2 = {h: _dot(jnp.concatenate([s1[h], kht_p[h // 2][sub(h), :]], axis=0), rhs2_p[h // 2])
          for h in hs_ids}
    z = {h: jnp.where(own[h % 2], at_p[h // 2], r2[h][:L]) for h in hs_ids}
    pw = {h: s1[h][:L, :n] for h in hs_ids}
    for it in range(steps):
        if it < steps - 1:
            w = {h: _dot(pw[h], jnp.concatenate([z[h], pw[h]], axis=1)) for h in hs_ids}
            z = {h: z[h] + w[h][:, :pw_] for h in hs_ids}
            pw = {h: w[h][:, pw_:] for h in hs_ids}
        else:
            z = {h: z[h] + _dot(pw[h], z[h]) for h in hs_ids}
    r9 = {h: _dot(jnp.concatenate([s1[h][L:, :n], bht_p[h // 2][sub(h), :]], axis=0), z[h])
          for h in hs_ids}
    r10 = {}
    for h in hs_ids:
        j, pi = h % 2, h // 2
        rh = rt_p[pi] + r9[h][:L]
        gm = r9[h][L:] + jnp.where(diag[j], p_last[:, psl[pi]], 0.0)
        hs = h_ref[h]
        rhs10 = jnp.concatenate([hs, zeros_np] if j == 0 else [zeros_np, hs], axis=0)
        r10[h] = _dot(jnp.concatenate([rh, gm], axis=0), rhs10)
    ys = []
    for pi in range(pairs):
        tot = {}
        for h in (2 * pi, 2 * pi + 1):
            tot[h] = r10[h] + r9[h] + r2[h][L:]
            h_ref[h] = jnp.where(own[h % 2][:n], 0.0, tot[h][L:])
        y_sw = jnp.where(own[0], tot[2 * pi + 1][:L], tot[2 * pi][:L])
        ys.append(pltpu.roll(y_sw, n, axis=1))
    y_ref[...] = jnp.concatenate(ys, axis=1).astype(y_ref.dtype)


def _rwkv_scan(r, lw, k, v, a, b, batch, seq):
    m, d = lw.shape
    nc = seq // CHUNK
    spec = pl.BlockSpec((CHUNK, d), lambda bi, ci: (bi * nc + ci, 0))
    heads = d // RWKV_HEAD
    return pl.pallas_call(
        _rwkv_scan_kernel,
        grid=(batch, nc),
        in_specs=[spec] * 6,
        out_specs=spec,
        out_shape=jax.ShapeDtypeStruct((m, d), BF16),
        scratch_shapes=[pltpu.VMEM((heads, RWKV_HEAD, 2 * RWKV_HEAD), F32)],
        compiler_params=_params(("arbitrary", "arbitrary")),
        name="rwkv_scan",
    )(r, lw, k, v, a, b)


def _rwkv_post_kernel(y_ref, r_ref, k_ref, v_ref, gate_ref, x_ref, lnw_ref, lnb_ref, rk_ref,
                      wo_ref, e_ref, et_ref, h_out):
    e = e_ref[...]
    et = et_ref[...]
    inv_n = 1.0 / RWKV_HEAD
    y = y_ref[...].astype(F32)
    mean = _dot_split(_dot(y, e) * inv_n, et)
    yc = y - mean
    var = _dot(yc * yc, e) * inv_n
    yn = yc * _dot_split(lax.rsqrt(var + GN_EPS), et) * lnw_ref[...] + lnb_ref[...]
    r = r_ref[...].astype(F32)
    k = k_ref[...].astype(F32)
    bonus = _dot_split(_dot(r * k * rk_ref[...], e), et) * v_ref[...].astype(F32)
    out = (yn + bonus) * gate_ref[...].astype(F32)
    h_out[...] = x_ref[...] + _dot(out, wo_ref[...])


def _rwkv_post(y, r, k, v, gate, x2, lnw, lnb, rk, wo, e, et, tm=512):
    m, d = y.shape
    row_spec = pl.BlockSpec((tm, d), lambda i: (i, 0))
    consts = (lnw, lnb, rk, wo, e, et)
    return pl.pallas_call(
        _rwkv_post_kernel,
        grid=(m // tm,),
        in_specs=[row_spec] * 6 + [_const_spec(c.shape) for c in consts],
        out_specs=row_spec,
        out_shape=jax.ShapeDtypeStruct((m, d), F32),
        compiler_params=_params(("arbitrary",)),
        name="rwkv_post",
    )(y, r, k, v, gate, x2, *consts)


def _mlp(h, g, wup_ref, wdn_ref):
    d = h.shape[1]
    xn = _rms(h, g).astype(BF16)
    acc = h
    for j in range(wup_ref.shape[1] // d):
        up = jnp.dot(xn, wup_ref[:, j * d:(j + 1) * d], preferred_element_type=F32)
        up = jnp.maximum(up, 0.0)
        acc = acc + jnp.dot((up * up).astype(BF16), wdn_ref[j * d:(j + 1) * d, :],
                            preferred_element_type=F32)
    return acc


def _mlp_gla_in_kernel(h_ref, gf_ref, wup_ref, wdn_ref, gm_ref, wq_ref, wk_ref, wv_ref, wg_ref,
                       wl_ref, wgk2_ref, bgk2_ref, h_out, q_out, k_out, v_out, og_out, gk_out):
    h2 = _mlp(h_ref[...], gf_ref[...], wup_ref, wdn_ref)
    h_out[...] = h2
    hn = _rms(h2, gm_ref[...]).astype(BF16)
    q_out[...] = jnp.dot(hn, wq_ref[...], preferred_element_type=F32).astype(q_out.dtype)
    k_out[...] = jnp.dot(hn, wk_ref[...], preferred_element_type=F32).astype(k_out.dtype)
    v_out[...] = jnp.dot(hn, wv_ref[...], preferred_element_type=F32).astype(v_out.dtype)
    og_out[...] = jnp.dot(hn, wg_ref[...], preferred_element_type=F32).astype(og_out.dtype)
    low = jnp.dot(hn, wl_ref[...], preferred_element_type=F32)
    z = _dot_split(low, wgk2_ref[...]) + bgk2_ref[...]
    gk_out[...] = jax.nn.log_sigmoid(z) * (1.0 / GLA_GATE_NORMALIZER)


def _mlp_gla_in(h1, gf, wup, wdn, gm, wq, wk, wv, wg, wl, wgk2, bgk2, tm=512):
    m, d = h1.shape
    dqk = wq.shape[1]
    row = lambda w: pl.BlockSpec((tm, w), lambda i: (i, 0))
    consts = (gf, wup, wdn, gm, wq, wk, wv, wg, wl, wgk2, bgk2)
    return pl.pallas_call(
        _mlp_gla_in_kernel,
        grid=(m // tm,),
        in_specs=[row(d)] + [_const_spec(c.shape) for c in consts],
        out_specs=[row(d), row(dqk), row(dqk), row(d), row(d), row(dqk)],
        out_shape=[jax.ShapeDtypeStruct((m, d), F32),
                   jax.ShapeDtypeStruct((m, dqk), BF16),
                   jax.ShapeDtypeStruct((m, dqk), BF16),
                   jax.ShapeDtypeStruct((m, d), BF16),
                   jax.ShapeDtypeStruct((m, d), BF16),
                   jax.ShapeDtypeStruct((m, dqk), F32)],
        compiler_params=_params(("arbitrary",)),
        name="mlp_gla_in",
    )(h1, *consts)


def _gla_scan_kernel(q_ref, k_ref, v_ref, g_ref, gn_ref, o_ref, s_ref, cum_ref, kf_ref):
    c = pl.program_id(1)

    @pl.when(c == 0)
    def _():
        s_ref[...] = jnp.zeros_like(s_ref)

    L = g_ref.shape[0]
    dk = g_ref.shape[1] // GLA_HEADS
    dv = v_ref.shape[1] // GLA_HEADS
    sub = GLA_SUB
    row = lax.broadcasted_iota(jnp.int32, (L, L), 0)
    col = lax.broadcasted_iota(jnp.int32, (L, L), 1)
    tri = jnp.where(row >= col, 1.0, 0.0).astype(BF16)
    diag_col = jnp.where(row // sub == col // sub, jnp.where(col <= row, col % sub, -1), -1)
    eye_k = (lax.broadcasted_iota(jnp.int32, (dk, dk), 0)
             == lax.broadcasted_iota(jnp.int32, (dk, dk), 1))

    cum = _cumsum_rows(g_ref[...], tri) * math.log2(math.e)
    cum_ref[...] = cum
    kf_ref[...] = k_ref[...].astype(F32)
    last = cum[L - 1:L, :]
    q_all = q_ref[...].astype(F32) * (dk ** -0.5)
    k_all = kf_ref[...]
    gn = gn_ref[...]

    levels = []
    size = L // 2
    while size >= sub:
        nblk = L // (2 * size)
        ref = jnp.concatenate(
            [jnp.broadcast_to(cum[(2 * b + 1) * size - 1:(2 * b + 1) * size, :], (2 * size, cum.shape[1]))
             for b in range(nblk)], axis=0)
        rb, cb = row // size, col // size
        levels.append((ref, jnp.where(rb % 2 == 1, rb - cb, 0) == 1))
        size //= 2

    for h in range(GLA_HEADS):
        sk = slice(h * dk, (h + 1) * dk)
        sv = slice(h * dv, (h + 1) * dv)
        q, k, g = q_all[:, sk], k_all[:, sk], cum[:, sk]
        scores = jnp.zeros((L, L), F32)
        for j in range(sub):
            kb = jnp.concatenate([jnp.broadcast_to(kf_ref[b * sub + j:b * sub + j + 1, sk], (sub, dk))
                                  for b in range(L // sub)], axis=0)
            gb = jnp.concatenate([jnp.broadcast_to(cum_ref[b * sub + j:b * sub + j + 1, sk], (sub, dk))
                                  for b in range(L // sub)], axis=0)
            e = jnp.exp2(jnp.minimum(g - gb, 0.0))
            dsum = jnp.sum(q * kb * e, axis=-1, keepdims=True)
            scores = jnp.where(diag_col == j, dsum, scores)
        for ref, mask in levels:
            rr = ref[:, sk]
            s_l = _dot_nt(q * jnp.exp2(jnp.minimum(g - rr, 0.0)), k * jnp.exp2(jnp.minimum(rr - g, 0.0)))
            scores = jnp.where(mask, s_l, scores)

        q_in = q * jnp.exp2(g)
        k_dec = k * jnp.exp2(last[:, sk] - g)
        decay = jnp.where(eye_k, jnp.exp2(last[:, sk]), 0.0)
        lhs = jnp.concatenate(
            [jnp.concatenate([q_in, scores], axis=1), jnp.concatenate([decay, k_dec.T], axis=1)],
            axis=0).astype(BF16)
        rhs = jnp.concatenate([s_ref[h].astype(BF16), v_ref[:, sv]], axis=0)
        res = jnp.dot(lhs, rhs, preferred_element_type=F32)
        s_ref[h] = res[L:]
        o = res[:L]
        o = o * lax.rsqrt(jnp.mean(o * o, axis=-1, keepdims=True) + NORM_EPS) * gn
        o_ref[:, sv] = o.astype(o_ref.dtype)


def _gla_scan(q, k, v, gk, gn, batch, seq):
    m, dqk = q.shape
    d = v.shape[1]
    nc = seq // CHUNK
    idx = lambda bi, ci: (bi * nc + ci, 0)
    return pl.pallas_call(
        _gla_scan_kernel,
        grid=(batch, nc),
        in_specs=[pl.BlockSpec((CHUNK, dqk), idx), pl.BlockSpec((CHUNK, dqk), idx),
                  pl.BlockSpec((CHUNK, d), idx), pl.BlockSpec((CHUNK, dqk), idx),
                  _const_spec(gn.shape)],
        out_specs=pl.BlockSpec((CHUNK, d), idx),
        out_shape=jax.ShapeDtypeStruct((m, d), BF16),
        scratch_shapes=[pltpu.VMEM((GLA_HEADS, dqk // GLA_HEADS, d // GLA_HEADS), F32),
                        pltpu.VMEM((CHUNK, dqk), F32), pltpu.VMEM((CHUNK, dqk), F32)],
        compiler_params=_params(("arbitrary", "arbitrary")),
        name="gla_scan",
    )(q, k, v, gk, gn)


def _gla_post_kernel(o_ref, og_ref, h_ref, wo_ref, gf_ref, wup_ref, wdn_ref, fg_ref, out_ref):
    og = og_ref[...].astype(F32)
    gated = o_ref[...].astype(F32) * (og * jax.nn.sigmoid(og))
    h3 = h_ref[...] + _dot(gated, wo_ref[...])
    h4 = _mlp(h3, gf_ref[...], wup_ref, wdn_ref)
    out_ref[...] = _rms(h4, fg_ref[...])


def _gla_post(o, og, h2, wo, gf, wup, wdn, fg, tm=512):
    m, d = h2.shape
    row_spec = pl.BlockSpec((tm, d), lambda i: (i, 0))
    consts = (wo, gf, wup, wdn, fg)
    return pl.pallas_call(
        _gla_post_kernel,
        grid=(m // tm,),
        in_specs=[row_spec] * 3 + [_const_spec(c.shape) for c in consts],
        out_specs=row_spec,
        out_shape=jax.ShapeDtypeStruct((m, d), F32),
        compiler_params=_params(("arbitrary",)),
        name="gla_post",
    )(o, og, h2, *consts)


def kernel(x, norm_mix_g, norm_ffn_g, mlp_up, mlp_down, rwkv_mu, rwkv_w_rkv, rwkv_w0, rwkv_w1, rwkv_w2, rwkv_a0, rwkv_a1, rwkv_a2, rwkv_g1, rwkv_g2, rwkv_k_k, rwkv_k_a, rwkv_r_k, rwkv_lnx_w, rwkv_lnx_b, rwkv_w_o, gla_w_in, gla_w_gk2, gla_b_gk2, gla_gnorm_g, gla_w_o, final_g):
    batch, seq, d = x.shape
    assert seq % CHUNK == 0 and d % RWKV_HEAD == 0
    m = batch * seq
    heads = d // RWKV_HEAD
    x2 = x.reshape(m, d)
    row = lambda t: t.reshape(1, -1).astype(F32)
    bf = lambda t: t.astype(BF16)
    head_of = jnp.arange(d, dtype=jnp.int32) // RWKV_HEAD
    e = (head_of[:, None] == jnp.arange(heads, dtype=jnp.int32)[None, :]).astype(BF16)
    et = e.T

    r, lw, kmod, v, a, b, gate = _rwkv_pre(
        x2, seq, row(norm_mix_g[0]), rwkv_mu[0], bf(rwkv_w_rkv[0]), row(rwkv_w0[0]),
        bf(rwkv_w1[0]), bf(rwkv_w2[0]), row(rwkv_a0[0]), bf(rwkv_a1[0]), bf(rwkv_a2[0]),
        bf(rwkv_g1[0]), bf(rwkv_g2[0]), row(rwkv_k_k[0]), row(rwkv_k_a[0]), e, et)
    y = _rwkv_scan(r, lw, kmod, v, a, b, batch, seq)
    h1 = _rwkv_post(y, r, kmod, v, gate, x2, row(rwkv_lnx_w[0]), row(rwkv_lnx_b[0]),
                    row(rwkv_r_k[0]), bf(rwkv_w_o[0]), e, et)

    w_in = gla_w_in[0]
    dqk = gla_w_gk2.shape[2]
    o_q, o_k, o_v, o_g = 0, dqk, 2 * dqk, 2 * dqk + d
    h2, q, k, vv, og, gk = _mlp_gla_in(
        h1, row(norm_ffn_g[0]), bf(mlp_up[0]), bf(mlp_down[0]), row(norm_mix_g[1]),
        bf(w_in[:, o_q:o_k]), bf(w_in[:, o_k:o_v]), bf(w_in[:, o_v:o_g]),
        bf(w_in[:, o_g:o_g + d]), bf(w_in[:, o_g + d:]), gla_w_gk2[0].astype(F32),
        row(gla_b_gk2[0]))
    o = _gla_scan(q, k, vv, gk, row(gla_gnorm_g[0]), batch, seq)
    out = _gla_post(o, og, h2, bf(gla_w_o[0]), row(norm_ffn_g[1]), bf(mlp_up[1]),
                    bf(mlp_down[1]), row(final_g))
    return out.reshape(batch, seq, d)
```

```python
import functools
import math

import jax
import jax.numpy as jnp
from jax import lax
from jax.experimental import pallas as pl
from jax.experimental.pallas import tpu as pltpu

F32 = jnp.float32
BF16 = jnp.bfloat16

CHUNK = 64
RWKV_HEAD = 64
GLA_HEADS = 4
GLA_GATE_NORMALIZER = 16.0
GLA_SUB = 8
NORM_EPS = 1e-5
GN_EPS = 1e-5 * RWKV_HEAD
V7X_VMEM_LIMIT_BYTES = 56 * 1024 * 1024
MXU_COLUMNS = 256
RWKV_SCAN_BATCHES = 4


def _dot(a, b):
    return jnp.dot(a.astype(BF16), b.astype(BF16), preferred_element_type=F32)


def _dot_nt(a, b):
    return lax.dot_general(a.astype(BF16), b.astype(BF16), (((1,), (1,)), ((), ())),
                           preferred_element_type=F32)


def _dot_split(a, b):
    hi = a.astype(BF16)
    lo = (a - hi.astype(F32)).astype(BF16)
    bb = b.astype(BF16)
    return (jnp.dot(hi, bb, preferred_element_type=F32)
            + jnp.dot(lo, bb, preferred_element_type=F32))


def _head_sums(x, ones_blk):
    w = ones_blk.shape[0]
    xb = x.astype(BF16)
    return jnp.concatenate(
        [jnp.dot(xb[:, g * w:(g + 1) * w], ones_blk, preferred_element_type=F32)
         for g in range(x.shape[1] // w)], axis=1)


def _rms(x, g):
    return x * lax.rsqrt(jnp.mean(x * x, axis=-1, keepdims=True) + NORM_EPS) * g


def _cumsum_rows(x, tri):
    hi = x.astype(BF16)
    lo = (x - hi.astype(F32)).astype(BF16)
    return (jnp.dot(tri, hi, preferred_element_type=F32)
            + jnp.dot(tri, lo, preferred_element_type=F32))


def _const_spec(shape):
    n = len(shape)
    return pl.BlockSpec(shape, lambda *_: (0,) * n)


def _params(sem):
    return pltpu.CompilerParams(dimension_semantics=sem,
                                vmem_limit_bytes=V7X_VMEM_LIMIT_BYTES)


def _rwkv_pre_kernel(x_ref, xh_ref, g_ref, mu_ref, wrkv_ref, w0_ref, w1_ref, w2_ref,
                     a0_ref, a1_ref, a2_ref, g1_ref, g2_ref, kk_ref, ka_ref, ones_ref,
                     r_out, lw_out, k_out, v_out, a_out, b_out, gate_out, *, tiles_per_seq):
    i = pl.program_id(0)
    g = g_ref[...]
    x = x_ref[...]
    tm = x.shape[0]
    hn = _rms(x, g)
    hp = _rms(xh_ref[...], g)[7:8, :]
    hp = jnp.where(i % tiles_per_seq == 0, 0.0, hp)
    row = lax.broadcasted_iota(jnp.int32, (tm, 1), 0)
    sh = jnp.where(row == 0, hp, pltpu.roll(hn, 1, axis=0))
    xx = sh - hn
    mu = mu_ref[...]

    def mix(n):
        return (hn + xx * mu[n:n + 1, :]).astype(BF16)

    r = jnp.dot(mix(0), wrkv_ref[0], preferred_element_type=F32)
    k = jnp.dot(mix(1), wrkv_ref[1], preferred_element_type=F32)
    v = jnp.dot(mix(2), wrkv_ref[2], preferred_element_type=F32)
    zw = w0_ref[...] + _dot(jnp.tanh(_dot(mix(3), w1_ref[...])), w2_ref[...])
    lw = -math.exp(-0.5) * jax.nn.sigmoid(zw)
    asig = jax.nn.sigmoid(a0_ref[...] + _dot(_dot(mix(4), a1_ref[...]), a2_ref[...]))
    gate = _dot(jax.nn.sigmoid(_dot(mix(5), g1_ref[...])), g2_ref[...])

    kk = k * kk_ref[...]
    kkn = kk * lax.rsqrt(jnp.maximum(_head_sums(kk * kk, ones_ref[...]), 1e-24))
    kmod = k * (1.0 + (asig - 1.0) * ka_ref[...])

    r_out[...] = r.astype(r_out.dtype)
    lw_out[...] = lw
    k_out[...] = kmod.astype(k_out.dtype)
    v_out[...] = v.astype(v_out.dtype)
    a_out[...] = (-kkn).astype(a_out.dtype)
    b_out[...] = (kkn * asig).astype(b_out.dtype)
    gate_out[...] = gate.astype(gate_out.dtype)


def _rwkv_pre(x2, seq, g, mu, wrkv, w0, w1, w2, a0, a1, a2, g1, g2, k_k, k_a, ones_blk, tm=512):
    m, d = x2.shape
    tiles_per_seq = seq // tm
    row_spec = pl.BlockSpec((tm, d), lambda i: (i, 0))
    halo_spec = pl.BlockSpec((8, d), lambda i: (jnp.maximum(i * (tm // 8) - 1, 0), 0))
    consts = (g, mu, wrkv, w0, w1, w2, a0, a1, a2, g1, g2, k_k, k_a, ones_blk)
    out_bf = jax.ShapeDtypeStruct((m, d), BF16)
    out_f32 = jax.ShapeDtypeStruct((m, d), F32)
    return pl.pallas_call(
        functools.partial(_rwkv_pre_kernel, tiles_per_seq=tiles_per_seq),
        grid=(m // tm,),
        in_specs=[row_spec, halo_spec] + [_const_spec(c.shape) for c in consts],
        out_specs=[row_spec] * 7,
        out_shape=[out_bf, out_f32, out_bf, out_bf, out_bf, out_bf, out_bf],
        compiler_params=_params(("arbitrary",)),
        name="rwkv_pre",
    )(x2, x2, *consts)


class _RwkvChunk:
    def __init__(self, L, d):
        n = RWKV_HEAD
        self.L, self.n, self.pw = L, n, 2 * n
        self.heads = d // n
        self.pairs = self.heads // 2
        self.psl = [slice(pi * 2 * n, (pi + 1) * 2 * n) for pi in range(self.pairs)]
        row = lax.broadcasted_iota(jnp.int32, (L, L), 0)
        col = lax.broadcasted_iota(jnp.int32, (L, L), 1)
        self.tri = jnp.where(row >= col, 1.0, 0.0).astype(BF16)
        lane = lax.broadcasted_iota(jnp.int32, (L, 2 * n), 1)
        self.own = [lane < n, lane >= n]
        row4 = lax.broadcasted_iota(jnp.int32, (2 * L, 2 * L), 0)
        col4 = lax.broadcasted_iota(jnp.int32, (2 * L, 2 * L), 1)
        self.mask4 = jnp.where(row4 < L, row4, row4 - L + 1) > col4 % L
        rowk = lax.broadcasted_iota(jnp.int32, (n, 2 * n), 0)
        lanek = lax.broadcasted_iota(jnp.int32, (n, 2 * n), 1)
        self.diag = [lanek == rowk, lanek == rowk + n]
        self.zeros_lp = jnp.zeros((L, 2 * n), F32)
        self.zeros_np = jnp.zeros((n, 2 * n), F32)

    def prologue(self, r, lw, k, v, a, b):
        L, n = self.L, self.n
        cl = _cumsum_rows(lw, self.tri)
        cl_last = cl[L - 1:L, :]
        ip = jnp.exp(-cl)
        pl_rel = jnp.exp(cl_last - cl)
        k = k.astype(F32)
        b = b.astype(F32)
        rt_all = r.astype(F32) * jnp.exp(cl)
        at_all = a.astype(F32) * jnp.exp(cl - lw)
        bt_all = (b * ip).astype(BF16)
        kt_all = (k * ip).astype(BF16)
        bh_all = b * pl_rel
        kh_all = k * pl_rel
        v_all = v.astype(F32)
        psl = self.psl
        return dict(
            p_last=jnp.exp(cl_last),
            at_p=[at_all[:, s] for s in psl],
            rt_p=[rt_all[:, s] for s in psl],
            yt_p=[jnp.concatenate([bt_all[:, s], kt_all[:, s]], axis=0) for s in psl],
            bkt_p=[jnp.concatenate([bh_all[:, s], kh_all[:, s]], axis=0).T for s in psl],
            vsw_p=[pltpu.roll(v_all[:, s], n, axis=1) for s in psl])

    def scores(self, c):
        L, own = self.L, self.own
        hs_ids = range(self.heads)
        s1 = {h: jnp.where(self.mask4, _dot_nt(
            jnp.concatenate([jnp.where(own[h % 2], c["at_p"][h // 2], 0.0),
                             jnp.where(own[h % 2], c["rt_p"][h // 2], 0.0)], axis=0),
            c["yt_p"][h // 2]), 0.0) for h in hs_ids}
        rhs2_p = [jnp.concatenate([self.zeros_lp, vs], axis=0).astype(BF16) for vs in c["vsw_p"]]
        aakv = {h: _dot(s1[h][:L], rhs2_p[h // 2]) for h in hs_ids}
        c["s1"] = s1
        c["z"] = {h: jnp.where(own[h % 2], c["at_p"][h // 2], aakv[h]) for h in hs_ids}

    def solve(self, c):
        L, n, pw_ = self.L, self.n, self.pw
        hs_ids = range(self.heads)
        z = c["z"]
        pw = {h: c["s1"][h][:L, :n] for h in hs_ids}
        steps = int(math.log2(L))
        for it in range(steps):
            if it < steps - 1:
                w = {h: _dot(pw[h], jnp.concatenate([z[h], pw[h]], axis=1)) for h in hs_ids}
                z = {h: z[h] + w[h][:, :pw_] for h in hs_ids}
                pw = {h: w[h][:, pw_:] for h in hs_ids}
            else:
                z = {h: z[h] + _dot(pw[h], z[h]) for h in hs_ids}
        c["z"] = z

    def output(self, c, h_ref, bi):
        L, n, own = self.L, self.n, self.own
        hs_ids = range(self.heads)
        r9 = {h: _dot(
            jnp.concatenate([c["s1"][h][L:], c["bkt_p"][h // 2][(h % 2) * n:(h % 2 + 1) * n, :]], axis=0),
            jnp.concatenate([c["z"][h], jnp.where(own[h % 2], 0.0, c["vsw_p"][h // 2])], axis=0))
              for h in hs_ids}
        r10 = {}
        for h in hs_ids:
            j, pi = h % 2, h // 2
            rh = c["rt_p"][pi] + r9[h][:L]
            gm = r9[h][L:] + jnp.where(self.diag[j], c["p_last"][:, self.psl[pi]], 0.0)
            hs = h_ref[bi, h]
            rhs10 = jnp.concatenate([hs, self.zeros_np] if j == 0 else [self.zeros_np, hs], axis=0)
            r10[h] = _dot(jnp.concatenate([rh, gm], axis=0), rhs10)
        ys = []
        for pi in range(self.pairs):
            tot = {}
            for h in (2 * pi, 2 * pi + 1):
                tot[h] = r10[h] + r9[h]
                h_ref[bi, h] = jnp.where(own[h % 2][:n], 0.0, tot[h][L:])
            y_sw = jnp.where(own[0], tot[2 * pi + 1][:L], tot[2 * pi][:L])
            ys.append(pltpu.roll(y_sw, n, axis=1))
        return jnp.concatenate(ys, axis=1)


def _rwkv_scan_kernel(r_ref, lw_ref, k_ref, v_ref, a_ref, b_ref, y_ref, h_ref):
    c = pl.program_id(1)

    @pl.when(c == 0)
    def _():
        h_ref[...] = jnp.zeros_like(h_ref)

    nb, L, d = lw_ref.shape
    ck = _RwkvChunk(L, d)
    st = [None] * nb
    for t in range(nb + 1):
        if t < nb:
            st[t] = ck.prologue(r_ref[t], lw_ref[t], k_ref[t], v_ref[t], a_ref[t], b_ref[t])
        if t >= 1:
            ck.solve(st[t - 1])
        if t < nb:
            ck.scores(st[t])
        if t >= 1:
            y_ref[t - 1] = ck.output(st[t - 1], h_ref, t - 1).astype(y_ref.dtype)
            st[t - 1] = None


def _rwkv_scan(r, lw, k, v, a, b, batch, seq):
    m, d = lw.shape
    nc = seq // CHUNK
    nb = RWKV_SCAN_BATCHES
    assert batch % nb == 0
    spec = pl.BlockSpec((nb, CHUNK, d), lambda bi, ci: (bi, ci, 0))
    heads = d // RWKV_HEAD
    to3 = lambda t: t.reshape(batch, seq, d)
    y = pl.pallas_call(
        _rwkv_scan_kernel,
        grid=(batch // nb, nc),
        in_specs=[spec] * 6,
        out_specs=spec,
        out_shape=jax.ShapeDtypeStruct((batch, seq, d), BF16),
        scratch_shapes=[pltpu.VMEM((nb, heads, RWKV_HEAD, 2 * RWKV_HEAD), F32)],
        compiler_params=_params(("arbitrary", "arbitrary")),
        name="rwkv_scan",
    )(to3(r), to3(lw), to3(k), to3(v), to3(a), to3(b))
    return y.reshape(m, d)


def _rwkv_post_kernel(y_ref, r_ref, k_ref, v_ref, gate_ref, x_ref, lnw_ref, lnb_ref, rk_ref,
                      wo_ref, ones_ref, h_out):
    ones_blk = ones_ref[...]
    inv_n = 1.0 / RWKV_HEAD
    y = y_ref[...].astype(F32)
    mean = _head_sums(y, ones_blk) * inv_n
    yc = y - mean
    var = _head_sums(yc * yc, ones_blk) * inv_n
    yn = yc * lax.rsqrt(var + GN_EPS) * lnw_ref[...] + lnb_ref[...]
    r = r_ref[...].astype(F32)
    k = k_ref[...].astype(F32)
    bonus = _head_sums(r * k * rk_ref[...], ones_blk) * v_ref[...].astype(F32)
    out = (yn + bonus) * gate_ref[...].astype(F32)
    h_out[...] = x_ref[...] + _dot(out, wo_ref[...])


def _rwkv_post(y, r, k, v, gate, x2, lnw, lnb, rk, wo, ones_blk, tm=512):
    m, d = y.shape
    row_spec = pl.BlockSpec((tm, d), lambda i: (i, 0))
    consts = (lnw, lnb, rk, wo, ones_blk)
    return pl.pallas_call(
        _rwkv_post_kernel,
        grid=(m // tm,),
        in_specs=[row_spec] * 6 + [_const_spec(c.shape) for c in consts],
        out_specs=row_spec,
        out_shape=jax.ShapeDtypeStruct((m, d), F32),
        compiler_params=_params(("arbitrary",)),
        name="rwkv_post",
    )(y, r, k, v, gate, x2, *consts)


def _mlp(h, g, wup_ref, wdn_ref):
    d = h.shape[1]
    xn = _rms(h, g).astype(BF16)
    acc = h
    for j in range(wup_ref.shape[1] // d):
        up = jnp.dot(xn, wup_ref[:, j * d:(j + 1) * d], preferred_element_type=F32)
        up = jnp.maximum(up, 0.0)
        acc = acc + jnp.dot((up * up).astype(BF16), wdn_ref[j * d:(j + 1) * d, :],
                            preferred_element_type=F32)
    return acc


def _mlp_gla_in_kernel(h_ref, gf_ref, wup_ref, wdn_ref, gm_ref, wq_ref, wk_ref, wv_ref, wg_ref,
                       wl_ref, wgk2_ref, bgk2_ref, h_out, q_out, k_out, v_out, og_out, gk_out):
    h2 = _mlp(h_ref[...], gf_ref[...], wup_ref, wdn_ref)
    h_out[...] = h2
    hn = _rms(h2, gm_ref[...]).astype(BF16)
    q_out[...] = jnp.dot(hn, wq_ref[...], preferred_element_type=F32).astype(q_out.dtype)
    k_out[...] = jnp.dot(hn, wk_ref[...], preferred_element_type=F32).astype(k_out.dtype)
    v_out[...] = jnp.dot(hn, wv_ref[...], preferred_element_type=F32).astype(v_out.dtype)
    og_out[...] = jnp.dot(hn, wg_ref[...], preferred_element_type=F32).astype(og_out.dtype)
    low = jnp.dot(hn, wl_ref[...], preferred_element_type=F32)
    z = _dot_split(low, wgk2_ref[...]) + bgk2_ref[...]
    gk_out[...] = jax.nn.log_sigmoid(z) * (1.0 / GLA_GATE_NORMALIZER)


def _mlp_gla_in(h1, gf, wup, wdn, gm, wq, wk, wv, wg, wl, wgk2, bgk2, tm=512):
    m, d = h1.shape
    dqk = wq.shape[1]
    row = lambda w: pl.BlockSpec((tm, w), lambda i: (i, 0))
    consts = (gf, wup, wdn, gm, wq, wk, wv, wg, wl, wgk2, bgk2)
    return pl.pallas_call(
        _mlp_gla_in_kernel,
        grid=(m // tm,),
        in_specs=[row(d)] + [_const_spec(c.shape) for c in consts],
        out_specs=[row(d), row(dqk), row(dqk), row(d), row(d), row(dqk)],
        out_shape=[jax.ShapeDtypeStruct((m, d), F32),
                   jax.ShapeDtypeStruct((m, dqk), BF16),
                   jax.ShapeDtypeStruct((m, dqk), BF16),
                   jax.ShapeDtypeStruct((m, d), BF16),
                   jax.ShapeDtypeStruct((m, d), BF16),
                   jax.ShapeDtypeStruct((m, dqk), F32)],
        compiler_params=_params(("arbitrary",)),
        name="mlp_gla_in",
    )(h1, *consts)


def _gla_scan_kernel(q_ref, k_ref, v_ref, g_ref, gn_ref, o_ref, s_ref, cum_ref, kf_ref):
    c = pl.program_id(1)

    @pl.when(c == 0)
    def _():
        s_ref[...] = jnp.zeros_like(s_ref)

    L = g_ref.shape[0]
    dk = g_ref.shape[1] // GLA_HEADS
    dv = v_ref.shape[1] // GLA_HEADS
    sub = GLA_SUB
    row = lax.broadcasted_iota(jnp.int32, (L, L), 0)
    col = lax.broadcasted_iota(jnp.int32, (L, L), 1)
    tri = jnp.where(row >= col, 1.0, 0.0).astype(BF16)
    diag_col = jnp.where(row // sub == col // sub, jnp.where(col <= row, col % sub, -1), -1)
    eye_k = (lax.broadcasted_iota(jnp.int32, (dk, dk), 0)
             == lax.broadcasted_iota(jnp.int32, (dk, dk), 1))

    cum = _cumsum_rows(g_ref[...], tri) * math.log2(math.e)
    cum_ref[...] = cum
    kf_ref[...] = k_ref[...].astype(F32)
    last = cum[L - 1:L, :]
    q_all = q_ref[...].astype(F32) * (dk ** -0.5)
    k_all = kf_ref[...]
    gn = gn_ref[...]

    levels = []
    size = L // 2
    while size >= sub:
        nblk = L // (2 * size)
        ref = jnp.concatenate(
            [jnp.broadcast_to(cum[(2 * b + 1) * size - 1:(2 * b + 1) * size, :], (2 * size, cum.shape[1]))
             for b in range(nblk)], axis=0)
        rb, cb = row // size, col // size
        levels.append((ref, jnp.where(rb % 2 == 1, rb - cb, 0) == 1))
        size //= 2

    for h in range(GLA_HEADS):
        sk = slice(h * dk, (h + 1) * dk)
        sv = slice(h * dv, (h + 1) * dv)
        q, k, g = q_all[:, sk], k_all[:, sk], cum[:, sk]
        scores = jnp.zeros((L, L), F32)
        for j in range(sub):
            kb = jnp.concatenate([jnp.broadcast_to(kf_ref[b * sub + j:b * sub + j + 1, sk], (sub, dk))
                                  for b in range(L // sub)], axis=0)
            gb = jnp.concatenate([jnp.broadcast_to(cum_ref[b * sub + j:b * sub + j + 1, sk], (sub, dk))
                                  for b in range(L // sub)], axis=0)
            e = jnp.exp2(jnp.minimum(g - gb, 0.0))
            dsum = jnp.sum(q * kb * e, axis=-1, keepdims=True)
            scores = jnp.where(diag_col == j, dsum, scores)
        for ref, mask in levels:
            rr = ref[:, sk]
            s_l = _dot_nt(q * jnp.exp2(jnp.minimum(g - rr, 0.0)), k * jnp.exp2(jnp.minimum(rr - g, 0.0)))
            scores = jnp.where(mask, s_l, scores)

        q_in = q * jnp.exp2(g)
        k_dec = k * jnp.exp2(last[:, sk] - g)
        decay = jnp.where(eye_k, jnp.exp2(last[:, sk]), 0.0)
        lhs = jnp.concatenate(
            [jnp.concatenate([q_in, scores], axis=1), jnp.concatenate([decay, k_dec.T], axis=1)],
            axis=0).astype(BF16)
        rhs = jnp.concatenate([s_ref[h].astype(BF16), v_ref[:, sv]], axis=0)
        res = jnp.dot(lhs, rhs, preferred_element_type=F32)
        s_ref[h] = res[L:]
        o = res[:L]
        o = o * lax.rsqrt(jnp.mean(o * o, axis=-1, keepdims=True) + NORM_EPS) * gn
        o_ref[:, sv] = o.astype(o_ref.dtype)


def _gla_scan(q, k, v, gk, gn, batch, seq):
    m, dqk = q.shape
    d = v.shape[1]
    nc = seq // CHUNK
    idx = lambda bi, ci: (bi * nc + ci, 0)
    return pl.pallas_call(
        _gla_scan_kernel,
        grid=(batch, nc),
        in_specs=[pl.BlockSpec((CHUNK, dqk), idx), pl.BlockSpec((CHUNK, dqk), idx),
                  pl.BlockSpec((CHUNK, d), idx), pl.BlockSpec((CHUNK, dqk), idx),
                  _const_spec(gn.shape)],
        out_specs=pl.BlockSpec((CHUNK, d), idx),
        out_shape=jax.ShapeDtypeStruct((m, d), BF16),
        scratch_shapes=[pltpu.VMEM((GLA_HEADS, dqk // GLA_HEADS, d // GLA_HEADS), F32),
                        pltpu.VMEM((CHUNK, dqk), F32), pltpu.VMEM((CHUNK, dqk), F32)],
        compiler_params=_params(("arbitrary", "arbitrary")),
        name="gla_scan",
    )(q, k, v, gk, gn)


def _gla_post_kernel(o_ref, og_ref, h_ref, wo_ref, gf_ref, wup_ref, wdn_ref, fg_ref, out_ref):
    og = og_ref[...].astype(F32)
    gated = o_ref[...].astype(F32) * (og * jax.nn.sigmoid(og))
    h3 = h_ref[...] + _dot(gated, wo_ref[...])
    h4 = _mlp(h3, gf_ref[...], wup_ref, wdn_ref)
    out_ref[...] = _rms(h4, fg_ref[...])


def _gla_post(o, og, h2, wo, gf, wup, wdn, fg, tm=512):
    m, d = h2.shape
    row_spec = pl.BlockSpec((tm, d), lambda i: (i, 0))
    consts = (wo, gf, wup, wdn, fg)
    return pl.pallas_call(
        _gla_post_kernel,
        grid=(m // tm,),
        in_specs=[row_spec] * 3 + [_const_spec(c.shape) for c in consts],
        out_specs=row_spec,
        out_shape=jax.ShapeDtypeStruct((m, d), F32),
        compiler_params=_params(("arbitrary",)),
        name="gla_post",
    )(o, og, h2, *consts)


def kernel(x, norm_mix_g, norm_ffn_g, mlp_up, mlp_down, rwkv_mu, rwkv_w_rkv, rwkv_w0, rwkv_w1, rwkv_w2, rwkv_a0, rwkv_a1, rwkv_a2, rwkv_g1, rwkv_g2, rwkv_k_k, rwkv_k_a, rwkv_r_k, rwkv_lnx_w, rwkv_lnx_b, rwkv_w_o, gla_w_in, gla_w_gk2, gla_b_gk2, gla_gnorm_g, gla_w_o, final_g):
    batch, seq, d = x.shape
    assert seq % CHUNK == 0 and d % RWKV_HEAD == 0
    m = batch * seq
    x2 = x.reshape(m, d)
    row = lambda t: t.reshape(1, -1).astype(F32)
    bf = lambda t: t.astype(BF16)
    head_of = jnp.arange(MXU_COLUMNS, dtype=jnp.int32) // RWKV_HEAD
    ones_blk = (head_of[:, None] == head_of[None, :]).astype(BF16)

    r, lw, kmod, v, a, b, gate = _rwkv_pre(
        x2, seq, row(norm_mix_g[0]), rwkv_mu[0], bf(rwkv_w_rkv[0]), row(rwkv_w0[0]),
        bf(rwkv_w1[0]), bf(rwkv_w2[0]), row(rwkv_a0[0]), bf(rwkv_a1[0]), bf(rwkv_a2[0]),
        bf(rwkv_g1[0]), bf(rwkv_g2[0]), row(rwkv_k_k[0]), row(rwkv_k_a[0]), ones_blk)
    y = _rwkv_scan(r, lw, kmod, v, a, b, batch, seq)
    h1 = _rwkv_post(y, r, kmod, v, gate, x2, row(rwkv_lnx_w[0]), row(rwkv_lnx_b[0]),
                    row(rwkv_r_k[0]), bf(rwkv_w_o[0]), ones_blk)

    w_in = gla_w_in[0]
    dqk = gla_w_gk2.shape[2]
    o_q, o_k, o_v, o_g = 0, dqk, 2 * dqk, 2 * dqk + d
    h2, q, k, vv, og, gk = _mlp_gla_in(
        h1, row(norm_ffn_g[0]), bf(mlp_up[0]), bf(mlp_down[0]), row(norm_mix_g[1]),
        bf(w_in[:, o_q:o_k]), bf(w_in[:, o_k:o_v]), bf(w_in[:, o_v:o_g]),
        bf(w_in[:, o_g:o_g + d]), bf(w_in[:, o_g + d:]), gla_w_gk2[0].astype(F32),
        row(gla_b_gk2[0]))
    o = _gla_scan(q, k, vv, gk, row(gla_gnorm_g[0]), batch, seq)
    out = _gla_post(o, og, h2, bf(gla_w_o[0]), row(norm_ffn_g[1]), bf(mlp_up[1]),
                    bf(mlp_down[1]), row(final_g))
    return out.reshape(batch, seq, d)
```

```python
import functools
import math

import jax
import jax.numpy as jnp
from jax import lax
from jax.experimental import pallas as pl
from jax.experimental.pallas import tpu as pltpu

F32 = jnp.float32
BF16 = jnp.bfloat16

CHUNK = 64
RWKV_HEAD = 64
GLA_HEADS = 4
GLA_GATE_NORMALIZER = 16.0
GLA_SUB = 8
NORM_EPS = 1e-5
GN_EPS = 1e-5 * RWKV_HEAD
V7X_VMEM_LIMIT_BYTES = 56 * 1024 * 1024
MXU_COLUMNS = 256
RWKV_SCAN_BATCHES = 4
GLA_SCAN_BATCHES = 2


def _dot(a, b):
    return jnp.dot(a.astype(BF16), b.astype(BF16), preferred_element_type=F32)


def _dot_nt(a, b):
    return lax.dot_general(a.astype(BF16), b.astype(BF16), (((1,), (1,)), ((), ())),
                           preferred_element_type=F32)


def _dot_split(a, b):
    hi = a.astype(BF16)
    lo = (a - hi.astype(F32)).astype(BF16)
    bb = b.astype(BF16)
    return (jnp.dot(hi, bb, preferred_element_type=F32)
            + jnp.dot(lo, bb, preferred_element_type=F32))


def _head_sums(x, ones_blk):
    w = ones_blk.shape[0]
    xb = x.astype(BF16)
    return jnp.concatenate(
        [jnp.dot(xb[:, g * w:(g + 1) * w], ones_blk, preferred_element_type=F32)
         for g in range(x.shape[1] // w)], axis=1)


def _rms(x, g):
    return x * lax.rsqrt(jnp.mean(x * x, axis=-1, keepdims=True) + NORM_EPS) * g


def _cumsum_rows(x, tri):
    hi = x.astype(BF16)
    lo = (x - hi.astype(F32)).astype(BF16)
    return (jnp.dot(tri, hi, preferred_element_type=F32)
            + jnp.dot(tri, lo, preferred_element_type=F32))


def _const_spec(shape):
    n = len(shape)
    return pl.BlockSpec(shape, lambda *_: (0,) * n)


def _params(sem):
    return pltpu.CompilerParams(dimension_semantics=sem,
                                vmem_limit_bytes=V7X_VMEM_LIMIT_BYTES)


def _rwkv_pre_kernel(x_ref, xh_ref, g_ref, mu_ref, wrkv_ref, w0_ref, w1_ref, w2_ref,
                     a0_ref, a1_ref, a2_ref, g1_ref, g2_ref, kk_ref, ka_ref, ones_ref,
                     r_out, lw_out, k_out, v_out, a_out, b_out, gate_out, *, tiles_per_seq):
    i = pl.program_id(0)
    g = g_ref[...]
    x = x_ref[...]
    tm = x.shape[0]
    hn = _rms(x, g)
    hp = _rms(xh_ref[...], g)[7:8, :]
    hp = jnp.where(i % tiles_per_seq == 0, 0.0, hp)
    row = lax.broadcasted_iota(jnp.int32, (tm, 1), 0)
    sh = jnp.where(row == 0, hp, pltpu.roll(hn, 1, axis=0))
    xx = sh - hn
    mu = mu_ref[...]

    def mix(n):
        return (hn + xx * mu[n:n + 1, :]).astype(BF16)

    r = jnp.dot(mix(0), wrkv_ref[0], preferred_element_type=F32)
    k = jnp.dot(mix(1), wrkv_ref[1], preferred_element_type=F32)
    v = jnp.dot(mix(2), wrkv_ref[2], preferred_element_type=F32)
    zw = w0_ref[...] + _dot(jnp.tanh(_dot(mix(3), w1_ref[...])), w2_ref[...])
    lw = -math.exp(-0.5) * jax.nn.sigmoid(zw)
    asig = jax.nn.sigmoid(a0_ref[...] + _dot(_dot(mix(4), a1_ref[...]), a2_ref[...]))
    gate = _dot(jax.nn.sigmoid(_dot(mix(5), g1_ref[...])), g2_ref[...])

    kk = k * kk_ref[...]
    kkn = kk * lax.rsqrt(jnp.maximum(_head_sums(kk * kk, ones_ref[...]), 1e-24))
    kmod = k * (1.0 + (asig - 1.0) * ka_ref[...])

    r_out[...] = r.astype(r_out.dtype)
    lw_out[...] = lw
    k_out[...] = kmod.astype(k_out.dtype)
    v_out[...] = v.astype(v_out.dtype)
    a_out[...] = (-kkn).astype(a_out.dtype)
    b_out[...] = (kkn * asig).astype(b_out.dtype)
    gate_out[...] = gate.astype(gate_out.dtype)


def _rwkv_pre(x2, seq, g, mu, wrkv, w0, w1, w2, a0, a1, a2, g1, g2, k_k, k_a, ones_blk, tm=512):
    m, d = x2.shape
    assert seq % tm == 0 and tm % 8 == 0
    tiles_per_seq = seq // tm
    row_spec = pl.BlockSpec((tm, d), lambda i: (i, 0))
    halo_spec = pl.BlockSpec((8, d), lambda i: (jnp.maximum(i * (tm // 8) - 1, 0), 0))
    consts = (g, mu, wrkv, w0, w1, w2, a0, a1, a2, g1, g2, k_k, k_a, ones_blk)
    out_bf = jax.ShapeDtypeStruct((m, d), BF16)
    out_f32 = jax.ShapeDtypeStruct((m, d), F32)
    return pl.pallas_call(
        functools.partial(_rwkv_pre_kernel, tiles_per_seq=tiles_per_seq),
        grid=(m // tm,),
        in_specs=[row_spec, halo_spec] + [_const_spec(c.shape) for c in consts],
        out_specs=[row_spec] * 7,
        out_shape=[out_bf, out_f32, out_bf, out_bf, out_bf, out_bf, out_bf],
        compiler_params=_params(("arbitrary",)),
        name="rwkv_pre",
    )(x2, x2, *consts)


class _RwkvChunk:
    def __init__(self, L, d):
        n = RWKV_HEAD
        self.L, self.n, self.pw = L, n, 2 * n
        self.heads = d // n
        self.pairs = self.heads // 2
        self.psl = [slice(pi * 2 * n, (pi + 1) * 2 * n) for pi in range(self.pairs)]
        row = lax.broadcasted_iota(jnp.int32, (L, L), 0)
        col = lax.broadcasted_iota(jnp.int32, (L, L), 1)
        self.tri = jnp.where(row >= col, 1.0, 0.0).astype(BF16)
        lane = lax.broadcasted_iota(jnp.int32, (L, 2 * n), 1)
        self.own = [lane < n, lane >= n]
        row4 = lax.broadcasted_iota(jnp.int32, (2 * L, 2 * L), 0)
        col4 = lax.broadcasted_iota(jnp.int32, (2 * L, 2 * L), 1)
        self.mask4 = jnp.where(row4 < L, row4, row4 - L + 1) > col4 % L
        rowk = lax.broadcasted_iota(jnp.int32, (n, 2 * n), 0)
        lanek = lax.broadcasted_iota(jnp.int32, (n, 2 * n), 1)
        self.diag = [lanek == rowk, lanek == rowk + n]
        self.zeros_lp = jnp.zeros((L, 2 * n), F32)
        self.zeros_np = jnp.zeros((n, 2 * n), F32)

    def prologue(self, r, lw, k, v, a, b):
        L, n = self.L, self.n
        cl = _cumsum_rows(lw, self.tri)
        cl_last = cl[L - 1:L, :]
        ip = jnp.exp(-cl)
        pl_rel = jnp.exp(cl_last - cl)
        k = k.astype(F32)
        b = b.astype(F32)
        rt_all = r.astype(F32) * jnp.exp(cl)
        at_all = a.astype(F32) * jnp.exp(cl - lw)
        bt_all = (b * ip).astype(BF16)
        kt_all = (k * ip).astype(BF16)
        bh_all = b * pl_rel
        kh_all = k * pl_rel
        v_all = v.astype(F32)
        psl = self.psl
        return dict(
            p_last=jnp.exp(cl_last),
            at_p=[at_all[:, s] for s in psl],
            rt_p=[rt_all[:, s] for s in psl],
            yt_p=[jnp.concatenate([bt_all[:, s], kt_all[:, s]], axis=0) for s in psl],
            bkt_p=[jnp.concatenate([bh_all[:, s], kh_all[:, s]], axis=0).T for s in psl],
            vsw_p=[pltpu.roll(v_all[:, s], n, axis=1) for s in psl])

    def scores(self, c):
        L, own = self.L, self.own
        hs_ids = range(self.heads)
        s1 = {h: jnp.where(self.mask4, _dot_nt(
            jnp.concatenate([jnp.where(own[h % 2], c["at_p"][h // 2], 0.0),
                             jnp.where(own[h % 2], c["rt_p"][h // 2], 0.0)], axis=0),
            c["yt_p"][h // 2]), 0.0) for h in hs_ids}
        rhs2_p = [jnp.concatenate([self.zeros_lp, vs], axis=0).astype(BF16) for vs in c["vsw_p"]]
        aakv = {h: _dot(s1[h][:L], rhs2_p[h // 2]) for h in hs_ids}
        c["s1"] = s1
        c["z"] = {h: jnp.where(own[h % 2], c["at_p"][h // 2], aakv[h]) for h in hs_ids}

    def solve(self, c):
        L, n, pw_ = self.L, self.n, self.pw
        hs_ids = range(self.heads)
        z = c["z"]
        pw = {h: c["s1"][h][:L, :n] for h in hs_ids}
        steps = int(math.log2(L))
        for it in range(steps):
            if it < steps - 1:
                w = {h: _dot(pw[h], jnp.concatenate([z[h], pw[h]], axis=1)) for h in hs_ids}
                z = {h: z[h] + w[h][:, :pw_] for h in hs_ids}
                pw = {h: w[h][:, pw_:] for h in hs_ids}
            else:
                z = {h: z[h] + _dot(pw[h], z[h]) for h in hs_ids}
        c["z"] = z

    def output(self, c, h_ref, bi):
        L, n, own = self.L, self.n, self.own
        hs_ids = range(self.heads)
        r9 = {h: _dot(
            jnp.concatenate([c["s1"][h][L:], c["bkt_p"][h // 2][(h % 2) * n:(h % 2 + 1) * n, :]], axis=0),
            jnp.concatenate([c["z"][h], jnp.where(own[h % 2], 0.0, c["vsw_p"][h // 2])], axis=0))
              for h in hs_ids}
        r10 = {}
        for h in hs_ids:
            j, pi = h % 2, h // 2
            rh = c["rt_p"][pi] + r9[h][:L]
            gm = r9[h][L:] + jnp.where(self.diag[j], c["p_last"][:, self.psl[pi]], 0.0)
            hs = h_ref[bi, h]
            rhs10 = jnp.concatenate([hs, self.zeros_np] if j == 0 else [self.zeros_np, hs], axis=0)
            r10[h] = _dot(jnp.concatenate([rh, gm], axis=0), rhs10)
        ys = []
        for pi in range(self.pairs):
            tot = {}
            for h in (2 * pi, 2 * pi + 1):
                tot[h] = r10[h] + r9[h]
                h_ref[bi, h] = jnp.where(own[h % 2][:n], 0.0, tot[h][L:])
            y_sw = jnp.where(own[0], tot[2 * pi + 1][:L], tot[2 * pi][:L])
            ys.append(pltpu.roll(y_sw, n, axis=1))
        return jnp.concatenate(ys, axis=1)


def _rwkv_scan_kernel(r_ref, lw_ref, k_ref, v_ref, a_ref, b_ref, y_ref, h_ref):
    c = pl.program_id(1)

    @pl.when(c == 0)
    def _():
        h_ref[...] = jnp.zeros_like(h_ref)

    nb, L, d = lw_ref.shape
    ck = _RwkvChunk(L, d)
    st = [None] * nb
    for t in range(nb + 1):
        if t < nb:
            st[t] = ck.prologue(r_ref[t], lw_ref[t], k_ref[t], v_ref[t], a_ref[t], b_ref[t])
        if t >= 1:
            ck.solve(st[t - 1])
        if t < nb:
            ck.scores(st[t])
        if t >= 1:
            y_ref[t - 1] = ck.output(st[t - 1], h_ref, t - 1).astype(y_ref.dtype)
            st[t - 1] = None


def _rwkv_scan(r, lw, k, v, a, b, batch, seq):
    m, d = lw.shape
    nc = seq // CHUNK
    nb = RWKV_SCAN_BATCHES
    assert batch % nb == 0
    spec = pl.BlockSpec((nb, CHUNK, d), lambda bi, ci: (bi, ci, 0))
    heads = d // RWKV_HEAD
    to3 = lambda t: t.reshape(batch, seq, d)
    y = pl.pallas_call(
        _rwkv_scan_kernel,
        grid=(batch // nb, nc),
        in_specs=[spec] * 6,
        out_specs=spec,
        out_shape=jax.ShapeDtypeStruct((batch, seq, d), BF16),
        scratch_shapes=[pltpu.VMEM((nb, heads, RWKV_HEAD, 2 * RWKV_HEAD), F32)],
        compiler_params=_params(("arbitrary", "arbitrary")),
        name="rwkv_scan",
    )(to3(r), to3(lw), to3(k), to3(v), to3(a), to3(b))
    return y.reshape(m, d)


def _rwkv_post_kernel(y_ref, r_ref, k_ref, v_ref, gate_ref, x_ref, lnw_ref, lnb_ref, rk_ref,
                      wo_ref, ones_ref, h_out):
    ones_blk = ones_ref[...]
    inv_n = 1.0 / RWKV_HEAD
    y = y_ref[...].astype(F32)
    mean = _head_sums(y, ones_blk) * inv_n
    yc = y - mean
    var = _head_sums(yc * yc, ones_blk) * inv_n
    yn = yc * lax.rsqrt(var + GN_EPS) * lnw_ref[...] + lnb_ref[...]
    r = r_ref[...].astype(F32)
    k = k_ref[...].astype(F32)
    bonus = _head_sums(r * k * rk_ref[...], ones_blk) * v_ref[...].astype(F32)
    out = (yn + bonus) * gate_ref[...].astype(F32)
    h_out[...] = x_ref[...] + _dot(out, wo_ref[...])


def _rwkv_post(y, r, k, v, gate, x2, lnw, lnb, rk, wo, ones_blk, tm=512):
    m, d = y.shape
    row_spec = pl.BlockSpec((tm, d), lambda i: (i, 0))
    consts = (lnw, lnb, rk, wo, ones_blk)
    return pl.pallas_call(
        _rwkv_post_kernel,
        grid=(m // tm,),
        in_specs=[row_spec] * 6 + [_const_spec(c.shape) for c in consts],
        out_specs=row_spec,
        out_shape=jax.ShapeDtypeStruct((m, d), F32),
        compiler_params=_params(("arbitrary",)),
        name="rwkv_post",
    )(y, r, k, v, gate, x2, *consts)


def _mlp(h, g, wup_ref, wdn_ref):
    d = h.shape[1]
    xn = _rms(h, g).astype(BF16)
    acc = h
    for j in range(wup_ref.shape[1] // d):
        up = jnp.dot(xn, wup_ref[:, j * d:(j + 1) * d], preferred_element_type=F32)
        up = jnp.maximum(up, 0.0)
        acc = acc + jnp.dot((up * up).astype(BF16), wdn_ref[j * d:(j + 1) * d, :],
                            preferred_element_type=F32)
    return acc


def _mlp_gla_in_kernel(h_ref, gf_ref, wup_ref, wdn_ref, gm_ref, wq_ref, wk_ref, wv_ref, wg_ref,
                       wl_ref, wgk2_ref, bgk2_ref, h_out, q_out, k_out, v_out, og_out, gk_out):
    h2 = _mlp(h_ref[...], gf_ref[...], wup_ref, wdn_ref)
    h_out[...] = h2
    hn = _rms(h2, gm_ref[...]).astype(BF16)
    q_out[...] = jnp.dot(hn, wq_ref[...], preferred_element_type=F32).astype(q_out.dtype)
    k_out[...] = jnp.dot(hn, wk_ref[...], preferred_element_type=F32).astype(k_out.dtype)
    v_out[...] = jnp.dot(hn, wv_ref[...], preferred_element_type=F32).astype(v_out.dtype)
    og_out[...] = jnp.dot(hn, wg_ref[...], preferred_element_type=F32).astype(og_out.dtype)
    low = jnp.dot(hn, wl_ref[...], preferred_element_type=F32)
    z = _dot_split(low, wgk2_ref[...]) + bgk2_ref[...]
    gk_out[...] = jax.nn.log_sigmoid(z) * (1.0 / GLA_GATE_NORMALIZER)


def _mlp_gla_in(h1, gf, wup, wdn, gm, wq, wk, wv, wg, wl, wgk2, bgk2, tm=512):
    m, d = h1.shape
    dqk = wq.shape[1]
    row = lambda w: pl.BlockSpec((tm, w), lambda i: (i, 0))
    consts = (gf, wup, wdn, gm, wq, wk, wv, wg, wl, wgk2, bgk2)
    return pl.pallas_call(
        _mlp_gla_in_kernel,
        grid=(m // tm,),
        in_specs=[row(d)] + [_const_spec(c.shape) for c in consts],
        out_specs=[row(d), row(dqk), row(dqk), row(d), row(d), row(dqk)],
        out_shape=[jax.ShapeDtypeStruct((m, d), F32),
                   jax.ShapeDtypeStruct((m, dqk), BF16),
                   jax.ShapeDtypeStruct((m, dqk), BF16),
                   jax.ShapeDtypeStruct((m, d), BF16),
                   jax.ShapeDtypeStruct((m, d), BF16),
                   jax.ShapeDtypeStruct((m, dqk), F32)],
        compiler_params=_params(("arbitrary",)),
        name="mlp_gla_in",
    )(h1, *consts)


def _gla_scan_kernel(q_ref, k_ref, v_ref, g_ref, gn_ref, o_ref, s_ref, cum_ref, kf_ref):
    c = pl.program_id(1)

    @pl.when(c == 0)
    def _():
        s_ref[...] = jnp.zeros_like(s_ref)

    nb, L, dqk = g_ref.shape
    dk = dqk // GLA_HEADS
    dv = v_ref.shape[2] // GLA_HEADS
    sub = GLA_SUB
    row = lax.broadcasted_iota(jnp.int32, (L, L), 0)
    col = lax.broadcasted_iota(jnp.int32, (L, L), 1)
    tri = jnp.where(row >= col, 1.0, 0.0).astype(BF16)
    diag_col = jnp.where(row // sub == col // sub, jnp.where(col <= row, col % sub, -1), -1)
    diag_masks = [diag_col == j for j in range(sub)]
    eye_k = (lax.broadcasted_iota(jnp.int32, (dk, dk), 0)
             == lax.broadcasted_iota(jnp.int32, (dk, dk), 1))
    level_masks = []
    size = L // 2
    while size >= sub:
        rb, cb = row // size, col // size
        level_masks.append((size, jnp.where(rb % 2 == 1, rb - cb, 0) == 1))
        size //= 2
    gn = gn_ref[...]

    for bi in range(nb):
        cum_ref[bi] = _cumsum_rows(g_ref[bi], tri) * math.log2(math.e)
        kf_ref[bi] = k_ref[bi].astype(F32)

    for bi in range(nb):
        cum = cum_ref[bi]
        last = cum[L - 1:L, :]
        q_all = q_ref[bi].astype(F32) * (dk ** -0.5)
        k_all = kf_ref[bi]
        refs = [jnp.concatenate(
            [jnp.broadcast_to(cum[(2 * b + 1) * size - 1:(2 * b + 1) * size, :], (2 * size, dqk))
             for b in range(L // (2 * size))], axis=0) for size, _ in level_masks]

        for h in range(GLA_HEADS):
            sk = slice(h * dk, (h + 1) * dk)
            sv = slice(h * dv, (h + 1) * dv)
            q, k, g = q_all[:, sk], k_all[:, sk], cum[:, sk]
            scores = jnp.zeros((L, L), F32)
            for j in range(sub):
                kb = jnp.concatenate(
                    [jnp.broadcast_to(kf_ref[bi, b * sub + j:b * sub + j + 1, sk], (sub, dk))
                     for b in range(L // sub)], axis=0)
                gb = jnp.concatenate(
                    [jnp.broadcast_to(cum_ref[bi, b * sub + j:b * sub + j + 1, sk], (sub, dk))
                     for b in range(L // sub)], axis=0)
                dsum = jnp.sum(q * kb * jnp.exp2(g - gb), axis=-1, keepdims=True)
                scores = jnp.where(diag_masks[j], dsum, scores)
            for ref, (_, mask) in zip(refs, level_masks):
                rr = ref[:, sk]
                s_l = _dot_nt(q * jnp.exp2(g - rr), k * jnp.exp2(rr - g))
                scores = jnp.where(mask, s_l, scores)

            q_in = q * jnp.exp2(g)
            k_dec = k * jnp.exp2(last[:, sk] - g)
            decay = jnp.where(eye_k, jnp.exp2(last[:, sk]), 0.0)
            lhs = jnp.concatenate(
                [jnp.concatenate([q_in, scores], axis=1), jnp.concatenate([decay, k_dec.T], axis=1)],
                axis=0).astype(BF16)
            rhs = jnp.concatenate([s_ref[bi, h].astype(BF16), v_ref[bi, :, sv]], axis=0)
            res = jnp.dot(lhs, rhs, preferred_element_type=F32)
            s_ref[bi, h] = res[L:]
            o = res[:L]
            o = o * lax.rsqrt(jnp.mean(o * o, axis=-1, keepdims=True) + NORM_EPS) * gn
            o_ref[bi, :, sv] = o.astype(o_ref.dtype)


def _gla_scan(q, k, v, gk, gn, batch, seq):
    m, dqk = q.shape
    d = v.shape[1]
    nc = seq // CHUNK
    nb = GLA_SCAN_BATCHES
    assert batch % nb == 0
    spec = lambda w: pl.BlockSpec((nb, CHUNK, w), lambda bi, ci: (bi, ci, 0))
    to3 = lambda t: t.reshape(batch, seq, t.shape[1])
    o = pl.pallas_call(
        _gla_scan_kernel,
        grid=(batch // nb, nc),
        in_specs=[spec(dqk), spec(dqk), spec(d), spec(dqk), _const_spec(gn.shape)],
        out_specs=spec(d),
        out_shape=jax.ShapeDtypeStruct((batch, seq, d), BF16),
        scratch_shapes=[pltpu.VMEM((nb, GLA_HEADS, dqk // GLA_HEADS, d // GLA_HEADS), F32),
                        pltpu.VMEM((nb, CHUNK, dqk), F32), pltpu.VMEM((nb, CHUNK, dqk), F32)],
        compiler_params=_params(("arbitrary", "arbitrary")),
        name="gla_scan",
    )(to3(q), to3(k), to3(v), to3(gk), gn)
    return o.reshape(m, d)


def _gla_post_kernel(o_ref, og_ref, h_ref, wo_ref, gf_ref, wup_ref, wdn_ref, fg_ref, out_ref):
    og = og_ref[...].astype(F32)
    gated = o_ref[...].astype(F32) * (og * jax.nn.sigmoid(og))
    h3 = h_ref[...] + _dot(gated, wo_ref[...])
    h4 = _mlp(h3, gf_ref[...], wup_ref, wdn_ref)
    out_ref[...] = _rms(h4, fg_ref[...])


def _gla_post(o, og, h2, wo, gf, wup, wdn, fg, tm=512):
    m, d = h2.shape
    row_spec = pl.BlockSpec((tm, d), lambda i: (i, 0))
    consts = (wo, gf, wup, wdn, fg)
    return pl.pallas_call(
        _gla_post_kernel,
        grid=(m // tm,),
        in_specs=[row_spec] * 3 + [_const_spec(c.shape) for c in consts],
        out_specs=row_spec,
        out_shape=jax.ShapeDtypeStruct((m, d), F32),
        compiler_params=_params(("arbitrary",)),
        name="gla_post",
    )(o, og, h2, *consts)


def kernel(x, norm_mix_g, norm_ffn_g, mlp_up, mlp_down, rwkv_mu, rwkv_w_rkv, rwkv_w0, rwkv_w1, rwkv_w2, rwkv_a0, rwkv_a1, rwkv_a2, rwkv_g1, rwkv_g2, rwkv_k_k, rwkv_k_a, rwkv_r_k, rwkv_lnx_w, rwkv_lnx_b, rwkv_w_o, gla_w_in, gla_w_gk2, gla_b_gk2, gla_gnorm_g, gla_w_o, final_g):
    batch, seq, d = x.shape
    assert seq % CHUNK == 0 and d % RWKV_HEAD == 0
    m = batch * seq
    x2 = x.reshape(m, d)
    row = lambda t: t.reshape(1, -1).astype(F32)
    bf = lambda t: t.astype(BF16)
    head_of = jnp.arange(MXU_COLUMNS, dtype=jnp.int32) // RWKV_HEAD
    ones_blk = (head_of[:, None] == head_of[None, :]).astype(BF16)

    r, lw, kmod, v, a, b, gate = _rwkv_pre(
        x2, seq, row(norm_mix_g[0]), rwkv_mu[0], bf(rwkv_w_rkv[0]), row(rwkv_w0[0]),
        bf(rwkv_w1[0]), bf(rwkv_w2[0]), row(rwkv_a0[0]), bf(rwkv_a1[0]), bf(rwkv_a2[0]),
        bf(rwkv_g1[0]), bf(rwkv_g2[0]), row(rwkv_k_k[0]), row(rwkv_k_a[0]), ones_blk)
    y = _rwkv_scan(r, lw, kmod, v, a, b, batch, seq)
    h1 = _rwkv_post(y, r, kmod, v, gate, x2, row(rwkv_lnx_w[0]), row(rwkv_lnx_b[0]),
                    row(rwkv_r_k[0]), bf(rwkv_w_o[0]), ones_blk)

    w_in = gla_w_in[0]
    dqk = gla_w_gk2.shape[2]
    o_q, o_k, o_v, o_g = 0, dqk, 2 * dqk, 2 * dqk + d
    h2, q, k, vv, og, gk = _mlp_gla_in(
        h1, row(norm_ffn_g[0]), bf(mlp_up[0]), bf(mlp_down[0]), row(norm_mix_g[1]),
        bf(w_in[:, o_q:o_k]), bf(w_in[:, o_k:o_v]), bf(w_in[:, o_v:o_g]),
        bf(w_in[:, o_g:o_g + d]), bf(w_in[:, o_g + d:]), gla_w_gk2[0].astype(F32),
        row(gla_b_gk2[0]))
    o = _gla_scan(q, k, vv, gk, row(gla_gnorm_g[0]), batch, seq)
    out = _gla_post(o, og, h2, bf(gla_w_o[0]), row(norm_ffn_g[1]), bf(mlp_up[1]),
                    bf(mlp_down[1]), row(final_g))
    return out.reshape(batch, seq, d)
```
